```python
import math
import jax
import jax.numpy as jnp
from jax import lax
import numpy as np

D_MODEL = 1024
BATCH = 16
SEQ = 4096
DEPTH = 4
DEC_BATCH = 32
DEC_SEQ = 2048
PAST_LEN = 128

GRID_W = 64
HEAD_DIM = 64
Q_BLOCK = 128
EPS = 1e-6
NEG_INF = -1e30

NA_HEADS = 4
NA_WIN_R = 8
NA_WIN_C = 16
GQA_HEADS = 4
GQA_KV_HEADS = 2
ROPE_THETA = 10000.0
DIFF_HEADS = 4
DIFF_QK_DIM = 32
DIFF_V_DIM = 2 * DIFF_QK_DIM
SWA_HEADS = 4
SWA_KV_HEADS = 2
SWA_WINDOW = 128

N_ALIBI_HEADS = DIFF_HEADS + SWA_HEADS
N_BRANCH = 4
BRANCH_W = 256
MIX_W = N_BRANCH * BRANCH_W
D_FF = 2816
CONV_W = 3

IN_SIZES = (NA_HEADS * HEAD_DIM, NA_HEADS * HEAD_DIM, NA_HEADS * HEAD_DIM,
            GQA_HEADS * HEAD_DIM, GQA_KV_HEADS * HEAD_DIM, GQA_KV_HEADS * HEAD_DIM,
            DIFF_HEADS * 2 * DIFF_QK_DIM, DIFF_HEADS * 2 * DIFF_QK_DIM, DIFF_HEADS * DIFF_V_DIM,
            SWA_HEADS * HEAD_DIM, SWA_KV_HEADS * HEAD_DIM, SWA_KV_HEADS * HEAD_DIM,
            N_BRANCH * D_MODEL)
IN_COLS = sum(IN_SIZES)

kernel_name = 'hybrid_gated_encoder'


def _rmsnorm(x, gain):
    xf = x.astype(jnp.float32)
    xf = xf * lax.rsqrt(jnp.mean(xf * xf, axis=-1, keepdims=True) + EPS)
    return (xf * gain.astype(jnp.float32)).astype(x.dtype)


def _alibi_slopes():
    s = 2.0 ** (-8.0 * np.arange(1, N_ALIBI_HEADS + 1) / N_ALIBI_HEADS)
    return jnp.asarray(s[0::2], dtype=jnp.float32), jnp.asarray(s[1::2], dtype=jnp.float32)


def _axial_rope_tables(S):
    t = jnp.arange(S)
    row = (t // GRID_W).astype(jnp.float32)
    col = (t % GRID_W).astype(jnp.float32)
    axis_dim = HEAD_DIM // 2
    inv = ROPE_THETA ** (-jnp.arange(0, axis_dim, 2, dtype=jnp.float32) / axis_dim)
    ang_r = row[:, None] * inv
    ang_c = col[:, None] * inv
    return jnp.cos(ang_r), jnp.sin(ang_r), jnp.cos(ang_c), jnp.sin(ang_c)


def _rotate(x, c, s):
    x1, x2 = jnp.split(x, 2, axis=-1)
    c = c[None, :, None, :]
    s = s[None, :, None, :]
    return jnp.concatenate([x1 * c - x2 * s, x2 * c + x1 * s], axis=-1)


def _apply_axial_rope(x, cos_r, sin_r, cos_c, sin_c):
    xf = x.astype(jnp.float32)
    xr, xc = jnp.split(xf, 2, axis=-1)
    return jnp.concatenate([_rotate(xr, cos_r, sin_r), _rotate(xc, cos_c, sin_c)], axis=-1).astype(x.dtype)


def _neighbourhood_attention(q, k, v, rpb):
    B, S, H, hd = q.shape
    rows = S // GRID_W
    win_r = min(NA_WIN_R, rows)
    grid = lambda t: t.reshape(B, rows, GRID_W, H, hd)
    qg, kg, vg = grid(q), grid(k), grid(v)
    cols = jnp.arange(GRID_W)
    col_start = jnp.clip(cols - NA_WIN_C // 2, 0, GRID_W - NA_WIN_C)
    col_idx = col_start[:, None] + jnp.arange(NA_WIN_C)[None, :]
    dc_idx = col_idx - cols[:, None] + NA_WIN_C - 1
    scale = hd ** -0.5

    def row_block(args):
        r, q_r = args
        r0 = jnp.clip(r - win_r // 2, 0, rows - win_r)
        k_rows = lax.dynamic_slice_in_dim(kg, r0, win_r, axis=1)
        v_rows = lax.dynamic_slice_in_dim(vg, r0, win_r, axis=1)
        k_nb = k_rows[:, :, col_idx]
        v_nb = v_rows[:, :, col_idx]
        logits = jnp.einsum('bqhd,brqchd->bhqrc', q_r, k_nb).astype(jnp.float32) * scale
        dr_idx = r0 + jnp.arange(win_r) - r + NA_WIN_R - 1
        bias = rpb[:, dr_idx[None, :, None], dc_idx[:, None, :]]
        logits = logits + bias.astype(jnp.float32)[None]
        p = jax.nn.softmax(logits.reshape(B, H, GRID_W, win_r * NA_WIN_C), axis=-1)
        p = p.reshape(B, H, GRID_W, win_r, NA_WIN_C).astype(v.dtype)
        return jnp.einsum('bhqrc,brqchd->bqhd', p, v_nb)

    out = lax.map(row_block, (jnp.arange(rows), jnp.moveaxis(qg, 1, 0)))
    return jnp.moveaxis(out, 0, 1).reshape(B, S, H * hd)


def _dense_gqa(q, k, v):
    B, S, H, hd = q.shape
    KVH = k.shape[2]
    G = H // KVH
    nb = S // Q_BLOCK
    qb = jnp.moveaxis(q.reshape(B, nb, Q_BLOCK, KVH, G, hd), 1, 0)
    scale = hd ** -0.5

    def blk(q_blk):
        logits = jnp.einsum('bqkgd,bskd->bkgqs', q_blk, k).astype(jnp.float32) * scale
        p = jax.nn.softmax(logits, axis=-1).astype(v.dtype)
        return jnp.einsum('bkgqs,bskd->bqkgd', p, v)

    out = lax.map(blk, qb)
    return jnp.moveaxis(out, 0, 1).reshape(B, S, H * hd)


def _diff_attention(q, k, v, lam, slopes):
    B, S, H, _, d = q.shape
    nb = S // Q_BLOCK
    qb = jnp.moveaxis(q.reshape(B, nb, Q_BLOCK, H, 2, d), 1, 0)
    key_pos = jnp.arange(S)
    scale = d ** -0.5

    def blk(args):
        i, q_blk = args
        q_pos = i * Q_BLOCK + jnp.arange(Q_BLOCK)
        dist = jnp.abs(q_pos[:, None] - key_pos[None, :]).astype(jnp.float32)
        logits = jnp.einsum('bqhjd,bshjd->bhjqs', q_blk, k).astype(jnp.float32) * scale
        logits = logits - slopes[None, :, None, None, None] * dist[None, None, None]
        p = jax.nn.softmax(logits, axis=-1)
        p = p[:, :, 0] - lam * p[:, :, 1]
        return jnp.einsum('bhqs,bshe->bqhe', p.astype(v.dtype), v)

    out = lax.map(blk, (jnp.arange(nb), qb))
    return jnp.moveaxis(out, 0, 1).reshape(B, S, H, 2 * d)


def _window_attention(q, k, v, sink, slopes):
    B, S, H, hd = q.shape
    KVH = k.shape[2]
    G = H // KVH
    nb = S // Q_BLOCK
    span = Q_BLOCK + 2 * SWA_WINDOW
    pad = ((0, 0), (SWA_WINDOW, SWA_WINDOW), (0, 0), (0, 0))
    kp = jnp.pad(k, pad)
    vp = jnp.pad(v, pad)
    qb = jnp.moveaxis(q.reshape(B, nb, Q_BLOCK, KVH, G, hd), 1, 0)
    sl = slopes.reshape(KVH, G)
    sk = sink.astype(jnp.float32).reshape(KVH, G)
    scale = hd ** -0.5

    def blk(args):
        i, q_blk = args
        start = i * Q_BLOCK
        k_blk = lax.dynamic_slice_in_dim(kp, start, span, axis=1)
        v_blk = lax.dynamic_slice_in_dim(vp, start, span, axis=1)
        q_pos = start + jnp.arange(Q_BLOCK)
        k_pos = start - SWA_WINDOW + jnp.arange(span)
        rel = jnp.abs(k_pos[None, :] - q_pos[:, None])
        valid = (rel <= SWA_WINDOW) & (k_pos >= 0)[None, :] & (k_pos < S)[None, :]
        logits = jnp.einsum('bqkgd,bskd->bkgqs', q_blk, k_blk).astype(jnp.float32) * scale
        logits = logits - sl[None, :, :, None, None] * rel.astype(jnp.float32)[None, None, None]
        logits = jnp.where(valid[None, None, None], logits, NEG_INF)
        sink_col = jnp.broadcast_to(sk[None, :, :, None, None], logits.shape[:-1] + (1,))
        p = jax.nn.softmax(jnp.concatenate([logits, sink_col], axis=-1), axis=-1)[..., :-1]
        return jnp.einsum('bkgqs,bskd->bqkgd', p.astype(v.dtype), v_blk)

    out = lax.map(blk, (jnp.arange(nb), qb))
    return jnp.moveaxis(out, 0, 1).reshape(B, S, H * hd)


def _conv_ffn(h, w_up, conv_w, conv_b, w_down):
    u = jnp.einsum('bsd,df->bsf', h, w_up)
    S = u.shape[1]
    p = CONV_W // 2
    up = jnp.pad(u, ((0, 0), (p, p), (0, 0)))
    c = conv_b
    for j in range(CONV_W):
        c = c + up[:, j:j + S] * conv_w[j]
    gate, val = jnp.split(c, 2, axis=-1)
    return jnp.einsum('bsf,fd->bsd', jax.nn.gelu(gate, approximate=True) * val, w_down)


def setup_inputs(seed: int = 0) -> dict:
    key = jax.random.key(seed)
    ks = jax.random.split(key, 24)
    f32 = jnp.float32
    nrm = lambda k, shape, s: s * jax.random.normal(k, shape, f32)
    gain = lambda k, shape: 1.0 + 0.05 * jax.random.normal(k, shape, f32)
    return {
        'x_prompt': nrm(ks[0], (BATCH, SEQ, D_MODEL), 1.0),
        'x_sample': nrm(ks[1], (DEC_BATCH, DEC_SEQ, D_MODEL), 1.0),
        'norm_mix_pre': gain(ks[2], (DEPTH, D_MODEL)),
        'norm_mix_post': gain(ks[3], (DEPTH, D_MODEL)),
        'norm_ffn_pre': gain(ks[4], (DEPTH, D_MODEL)),
        'norm_ffn_post': gain(ks[5], (DEPTH, D_MODEL)),
        'w_in': nrm(ks[6], (DEPTH, D_MODEL, IN_COLS), D_MODEL ** -0.5),
        'na_rpb': nrm(ks[7], (DEPTH, NA_HEADS, 2 * NA_WIN_R - 1, 2 * NA_WIN_C - 1), 0.1),
        'gqa_q_norm': gain(ks[8], (DEPTH, HEAD_DIM)),
        'gqa_k_norm': gain(ks[9], (DEPTH, HEAD_DIM)),
        'diff_lambda_q1': nrm(ks[10], (DEPTH, DIFF_QK_DIM), 0.1),
        'diff_lambda_k1': nrm(ks[11], (DEPTH, DIFF_QK_DIM), 0.1),
        'diff_lambda_q2': nrm(ks[12], (DEPTH, DIFF_QK_DIM), 0.1),
        'diff_lambda_k2': nrm(ks[13], (DEPTH, DIFF_QK_DIM), 0.1),
        'diff_subln': gain(ks[14], (DEPTH, DIFF_V_DIM)),
        'swa_sink': nrm(ks[15], (DEPTH, SWA_HEADS), 0.5),
        'w_branch': nrm(ks[16], (DEPTH, MIX_W, D_MODEL), BRANCH_W ** -0.5),
        'w_out': nrm(ks[17], (DEPTH, D_MODEL, D_MODEL), D_MODEL ** -0.5),
        'ffn_w_up': nrm(ks[18], (DEPTH, D_MODEL, 2 * D_FF), D_MODEL ** -0.5),
        'ffn_conv_w': nrm(ks[19], (DEPTH, CONV_W, 2 * D_FF), 0.5),
        'ffn_conv_b': nrm(ks[20], (DEPTH, 2 * D_FF), 0.01),
        'ffn_w_down': nrm(ks[21], (DEPTH, D_FF, D_MODEL), D_FF ** -0.5),
    }


def reference(x_prompt, x_sample, norm_mix_pre, norm_mix_post, norm_ffn_pre, norm_ffn_post,
              w_in, na_rpb, gqa_q_norm, gqa_k_norm, diff_lambda_q1, diff_lambda_k1,
              diff_lambda_q2, diff_lambda_k2, diff_subln, swa_sink, w_branch, w_out,
              ffn_w_up, ffn_conv_w, ffn_conv_b, ffn_w_down):
    diff_slopes, swa_slopes = _alibi_slopes()
    split_points = tuple(int(s) for s in np.cumsum(IN_SIZES)[:-1])

    def run_trunk(x):
        B, S, _ = x.shape
        cos_r, sin_r, cos_c, sin_c = _axial_rope_tables(S)
        heads = lambda t, n: t.reshape(B, S, n, HEAD_DIM)
        for l in range(DEPTH):
            lambda_init = 0.8 - 0.6 * math.exp(-0.3 * l)
            h = _rmsnorm(x, norm_mix_pre[l])
            proj = jnp.einsum('bsd,de->bse', h, w_in[l])
            (a_q, a_k, a_v, b_q, b_k, b_v, c_q, c_k, c_v,
             d_q, d_k, d_v, g) = jnp.split(proj, split_points, axis=-1)
            o_a = _neighbourhood_attention(heads(a_q, NA_HEADS), heads(a_k, NA_HEADS),
                                           heads(a_v, NA_HEADS), na_rpb[l])
            qb = _apply_axial_rope(_rmsnorm(heads(b_q, GQA_HEADS), gqa_q_norm[l]), cos_r, sin_r, cos_c, sin_c)
            kb = _apply_axial_rope(_rmsnorm(heads(b_k, GQA_KV_HEADS), gqa_k_norm[l]), cos_r, sin_r, cos_c, sin_c)
            o_b = _dense_gqa(qb, kb, heads(b_v, GQA_KV_HEADS))
            lam = (jnp.exp(jnp.sum(diff_lambda_q1[l].astype(jnp.float32) * diff_lambda_k1[l].astype(jnp.float32)))
                   - jnp.exp(jnp.sum(diff_lambda_q2[l].astype(jnp.float32) * diff_lambda_k2[l].astype(jnp.float32)))
                   + lambda_init)
            o_c = _diff_attention(c_q.reshape(B, S, DIFF_HEADS, 2, DIFF_QK_DIM),
                                  c_k.reshape(B, S, DIFF_HEADS, 2, DIFF_QK_DIM),
                                  c_v.reshape(B, S, DIFF_HEADS, DIFF_V_DIM), lam, diff_slopes)
            o_c = (_rmsnorm(o_c, diff_subln[l]) * (1.0 - lambda_init)).reshape(B, S, BRANCH_W)
            o_d = _window_attention(heads(d_q, SWA_HEADS), heads(d_k, SWA_KV_HEADS),
                                    heads(d_v, SWA_KV_HEADS), swa_sink[l], swa_slopes)
            gates = jax.nn.sigmoid(g.reshape(B, S, N_BRANCH, D_MODEL))
            wb = w_branch[l].reshape(N_BRANCH, BRANCH_W, D_MODEL)
            branches = (o_a, o_b, o_c, o_d)
            merged = gates[:, :, 0] * jnp.einsum('bsc,cd->bsd', branches[0], wb[0])
            for i in range(1, N_BRANCH):
                merged = merged + gates[:, :, i] * jnp.einsum('bsc,cd->bsd', branches[i], wb[i])
            mix = jnp.einsum('bsd,de->bse', merged, w_out[l])
            x = x + _rmsnorm(mix, norm_mix_post[l])
            hf = _rmsnorm(x, norm_ffn_pre[l])
            f = _conv_ffn(hf, ffn_w_up[l], ffn_conv_w[l], ffn_conv_b[l], ffn_w_down[l])
            x = x + _rmsnorm(f, norm_ffn_post[l])
        return x

    y_prompt = run_trunk(x_prompt)
    y_sample = run_trunk(x_sample)
    return (y_prompt, y_sample)
```

```python
import functools
import math

import numpy as np
import jax
import jax.numpy as jnp
from jax import lax
from jax.experimental import pallas as pl
from jax.experimental.pallas import tpu as pltpu

F32 = jnp.float32
BF16 = jnp.bfloat16

D_MODEL = 1024
GRID_W = 64
HEAD_DIM = 64
EPS = 1e-6
NEG_INF = -1e30
NA_WIN_R = 8
NA_WIN_C = 16
ROPE_THETA = 10000.0
DIFF_QK_DIM = 32
SWA_WINDOW = 128
N_ALIBI_HEADS = 8
N_BRANCH = 4
BRANCH_W = 256
D_FF = 2816
QKV_COLS = 2560

A_Q, A_K, A_V = 0, 256, 512
B_Q, B_K, B_V = 768, 1024, 1152
C_Q, C_K, C_V = 1280, 1536, 1792
D_Q, D_K, D_V = 2048, 2304, 2432

LANES = 128
VMEM_LIMIT = 56 * 1024 * 1024

TOKEN_TILE = 512
HALO = 16
FFN_CHUNK = 256
NA_Q_ROWS = 4
NA_K_ROWS = 12
GQA_TQ, GQA_TK = 256, 512
DIFF_TQ, DIFF_TK = 128, 512
SWA_TQ = 256


def _alibi_slopes():
    s = 2.0 ** (-8.0 * np.arange(1, N_ALIBI_HEADS + 1) / N_ALIBI_HEADS)
    return [float(v) for v in s[0::2]], [float(v) for v in s[1::2]]


DIFF_SLOPES, SWA_SLOPES = _alibi_slopes()


def _params(n_parallel):
    return pltpu.CompilerParams(dimension_semantics=("parallel",) * n_parallel,
                                vmem_limit_bytes=VMEM_LIMIT)


def _whole(space=pltpu.VMEM):
    return pl.BlockSpec(memory_space=space)


def _rmsnorm(x, gain):
    ms = jnp.mean(x * x, axis=-1, keepdims=True)
    return x * lax.rsqrt(ms + EPS) * gain


def _pair_rms_scale(y):
    lo = lax.broadcasted_iota(jnp.int32, y.shape, 1) < HEAD_DIM
    y2 = y * y
    s_lo = jnp.sum(jnp.where(lo, y2, 0.0), axis=-1, keepdims=True)
    s_hi = jnp.sum(jnp.where(lo, 0.0, y2), axis=-1, keepdims=True)
    ms = jnp.where(lo, s_lo, s_hi) * (1.0 / HEAD_DIM)
    return lax.rsqrt(ms + EPS)


def _nt_dot(a, b):
    return lax.dot_general(a, b, (((1,), (1,)), ((), ())), preferred_element_type=F32)


def _in_proj_kernel(x_ref, gain_ref, w_ref, cos_ref, sin_ref, qn_ref, kn_ref, o_ref):
    h = _rmsnorm(x_ref[...], gain_ref[...]).astype(BF16)

    def proj(c0, width):
        return jnp.dot(h, w_ref[:, c0:c0 + width], preferred_element_type=F32)

    def plain(c0, width, scale=None):
        y = proj(c0, width)
        if scale is not None:
            y = y * scale
        o_ref[:, c0:c0 + width] = y.astype(BF16)

    def normed_rope(c0, gain_ref_, scale=None):
        y = proj(c0, LANES)
        yn = y * _pair_rms_scale(y) * gain_ref_[...]
        lane = lax.broadcasted_iota(jnp.int32, yn.shape, 1)
        partner = jnp.where((lane % 32) < 16, pltpu.roll(yn, LANES - 16, 1), pltpu.roll(yn, 16, 1))
        out = yn * cos_ref[...] + partner * sin_ref[...]
        if scale is not None:
            out = out * scale
        o_ref[:, c0:c0 + LANES] = out.astype(BF16)

    hd_scale = HEAD_DIM ** -0.5
    plain(A_Q, 256, hd_scale)
    plain(A_K, 512)
    normed_rope(B_Q, qn_ref, hd_scale)
    normed_rope(B_Q + LANES, qn_ref, hd_scale)
    normed_rope(B_K, kn_ref)
    plain(B_V, 128)
    plain(C_Q, 256, DIFF_QK_DIM ** -0.5)
    plain(C_K, 512)
    plain(D_Q, 256, hd_scale)
    plain(D_K, 256)


def _in_proj(x2d, gain, w_qkv, cos_t, sin_t, qn, kn, seq):
    n_tok = x2d.shape[0]
    t = TOKEN_TILE
    tiles_per_seq = seq // t
    return pl.pallas_call(
        _in_proj_kernel,
        grid=(n_tok // t,),
        in_specs=[
            pl.BlockSpec((t, D_MODEL), lambda i: (i, 0)),
            _whole(), _whole(),
            pl.BlockSpec((t, LANES), lambda i: (i % tiles_per_seq, 0)),
            pl.BlockSpec((t, LANES), lambda i: (i % tiles_per_seq, 0)),
            _whole(), _whole(),
        ],
        out_specs=pl.BlockSpec((t, QKV_COLS), lambda i: (i, 0)),
        out_shape=jax.ShapeDtypeStruct((n_tok, QKV_COLS), BF16),
        compiler_params=_params(1),
    )(x2d, gain, w_qkv, cos_t, sin_t, qn, kn)


def _na_kernel(q_ref, k_ref, v_ref, b_ref, o_ref, *, rows):
    t = pl.program_id(1)
    tq = NA_Q_ROWS * GRID_W
    tk = NA_K_ROWS * GRID_W
    r_start = jnp.clip(t * NA_Q_ROWS - NA_WIN_R // 2, 0, rows - NA_K_ROWS)
    k0 = pl.multiple_of(r_start * GRID_W, GRID_W)
    q = q_ref[0]
    kw = k_ref[0, pl.ds(k0, tk), :]
    vw = v_ref[0, pl.ds(k0, tk), :]
    lo = lax.broadcasted_iota(jnp.int32, (tq, LANES), 1) < HEAD_DIM
    for pair in range(2):
        sl = slice(pair * LANES, (pair + 1) * LANES)
        qp, kp, vp = q[:, sl], kw[:, sl], vw[:, sl]
        halves = []
        for half in range(2):
            sel = lo if half == 0 else jnp.logical_not(lo)
            qm = jnp.where(sel, qp, jnp.zeros_like(qp))
            s = _nt_dot(qm, kp) + b_ref[0, 2 * pair + half]
            m = jnp.max(s, axis=-1, keepdims=True)
            e = jnp.exp(s - m)
            l = jnp.sum(e, axis=-1, keepdims=True)
            halves.append(jnp.dot(e.astype(BF16), vp, preferred_element_type=F32) / l)
        o_ref[0, :, sl] = jnp.where(lo, halves[0], halves[1]).astype(BF16)


def _na_bias_tables(rpb, rows):
    tq, tk = NA_Q_ROWS * GRID_W, NA_K_ROWS * GRID_W
    iq, ik = np.arange(tq)[:, None], np.arange(tk)[None, :]
    cases = []
    for q_row0 in (0, NA_Q_ROWS, rows - NA_Q_ROWS):
        r_start = int(np.clip(q_row0 - NA_WIN_R // 2, 0, rows - NA_K_ROWS))
        qr, qc = q_row0 + iq // GRID_W, iq % GRID_W
        kr, kc = r_start + ik // GRID_W, ik % GRID_W
        r0 = np.clip(qr - NA_WIN_R // 2, 0, rows - NA_WIN_R)
        c0 = np.clip(qc - NA_WIN_C // 2, 0, GRID_W - NA_WIN_C)
        valid = (kr >= r0) & (kr < r0 + NA_WIN_R) & (kc >= c0) & (kc < c0 + NA_WIN_C)
        dr = np.clip(kr - qr + NA_WIN_R - 1, 0, 2 * NA_WIN_R - 2)
        dc = np.clip(kc - qc + NA_WIN_C - 1, 0, 2 * NA_WIN_C - 2)
        cases.append(jnp.where(valid[None, None], rpb[:, :, dr, dc], NEG_INF))
    return jnp.stack(cases, axis=1).astype(F32)


def _na_attention(qkv, bias, batch, seq):
    rows = seq // GRID_W
    n_t = rows // NA_Q_ROWS
    tq, tk = NA_Q_ROWS * GRID_W, NA_K_ROWS * GRID_W
    case = lambda t: jnp.where(t == 0, 0, jnp.where(t == n_t - 1, 2, 1))
    return pl.pallas_call(
        functools.partial(_na_kernel, rows=rows),
        grid=(batch, n_t),
        in_specs=[
            pl.BlockSpec((1, tq, 256), lambda b, t: (b, t, A_Q // 256)),
            pl.BlockSpec((1, seq, 256), lambda b, t: (b, 0, A_K // 256)),
            pl.BlockSpec((1, seq, 256), lambda b, t: (b, 0, A_V // 256)),
            pl.BlockSpec((1, 4, tq, tk), lambda b, t: (case(t), 0, 0, 0)),
        ],
        out_specs=pl.BlockSpec((1, tq, BRANCH_W), lambda b, t: (b, t, 0)),
        out_shape=jax.ShapeDtypeStruct((batch, seq, BRANCH_W), BF16),
        compiler_params=_params(2),
    )(qkv, qkv, qkv, bias)


def _online_softmax_step(s, v, carry):
    m, l, acc = carry
    m_new = jnp.maximum(m, jnp.max(s, axis=-1, keepdims=True))
    alpha = jnp.exp(m - m_new)
    p = jnp.exp(s - m_new)
    l = alpha * l + jnp.sum(p, axis=-1, keepdims=True)
    acc = alpha * acc + jnp.dot(p.astype(BF16), v, preferred_element_type=F32)
    return m_new, l, acc


def _softmax_init(rows):
    return (jnp.full((rows, 1), NEG_INF, F32), jnp.zeros((rows, 1), F32), jnp.zeros((rows, LANES), F32))


def _stack_group(q, sel):
    g0, g1 = q[:, :LANES], q[:, LANES:]
    zero = jnp.zeros_like(g0)
    return jnp.concatenate([jnp.where(sel, g0, zero), jnp.where(sel, g1, zero)], axis=0)


def _gqa_kernel(q_ref, k_ref, v_ref, o_ref, *, seq):
    tq, tk = GQA_TQ, GQA_TK
    q = q_ref[0]
    lo = lax.broadcasted_iota(jnp.int32, (tq, LANES), 1) < HEAD_DIM
    outs = []
    for kv_head in range(2):
        qq = _stack_group(q, lo if kv_head == 0 else jnp.logical_not(lo))

        def body(j, carry, qq=qq):
            k0 = pl.multiple_of(j * tk, tk)
            s = _nt_dot(qq, k_ref[0, pl.ds(k0, tk), :])
            return _online_softmax_step(s, v_ref[0, pl.ds(k0, tk), :], carry)

        _, l, acc = lax.fori_loop(0, seq // tk, body, _softmax_init(2 * tq))
        outs.append(acc / l)
    o_ref[0, :, :LANES] = jnp.where(lo, outs[0][:tq], outs[1][:tq]).astype(BF16)
    o_ref[0, :, LANES:] = jnp.where(lo, outs[0][tq:], outs[1][tq:]).astype(BF16)


def _gqa_attention(qkv, batch, seq):
    tq = GQA_TQ
    return pl.pallas_call(
        functools.partial(_gqa_kernel, seq=seq),
        grid=(batch, seq // tq),
        in_specs=[
            pl.BlockSpec((1, tq, 256), lambda b, t: (b, t, B_Q // 256)),
            pl.BlockSpec((1, seq, LANES), lambda b, t: (b, 0, B_K // LANES)),
            pl.BlockSpec((1, seq, LANES), lambda b, t: (b, 0, B_V // LANES)),
        ],
        out_specs=pl.BlockSpec((1, tq, BRANCH_W), lambda b, t: (b, t, 0)),
        out_shape=jax.ShapeDtypeStruct((batch, seq, BRANCH_W), BF16),
        compiler_params=_params(2),
    )(qkv, qkv, qkv)


def _diff_kernel(q_ref, k_ref, v_ref, lq1_ref, lk1_ref, lq2_ref, lk2_ref, subln_ref, o_ref, *,
                 seq, lambda_init):
    tq, tk = DIFF_TQ, DIFF_TK
    q_start = pl.program_id(1) * tq
    q = q_ref[0]
    lam = (jnp.exp(jnp.sum(lq1_ref[...] * lk1_ref[...], axis=-1, keepdims=True))
           - jnp.exp(jnp.sum(lq2_ref[...] * lk2_ref[...], axis=-1, keepdims=True)) + lambda_init)
    lane = lax.broadcasted_iota(jnp.int32, (tq, LANES), 1)
    seg = lane // DIFF_QK_DIM
    lo = lane < HEAD_DIM
    rel0 = (lax.broadcasted_iota(jnp.int32, (tq, tk), 0) - lax.broadcasted_iota(jnp.int32, (tq, tk), 1))
    row = lax.broadcasted_iota(jnp.int32, (4 * tq, 1), 0)
    for pair in range(2):
        sl = slice(pair * LANES, (pair + 1) * LANES)
        qp = q[:, sl]
        zero = jnp.zeros_like(qp)
        qq = jnp.concatenate([jnp.where(seg == i, qp, zero) for i in range(4)], axis=0)
        slope = jnp.where(row < 2 * tq, DIFF_SLOPES[2 * pair], DIFF_SLOPES[2 * pair + 1]).astype(F32)

        def body(j, carry, qq=qq, slope=slope, sl=sl):
            k0 = pl.multiple_of(j * tk, tk)
            s = _nt_dot(qq, k_ref[0, pl.ds(k0, tk), sl])
            dist = jnp.abs(rel0 + (q_start - k0)).astype(F32)
            s = s - slope * jnp.concatenate([dist] * 4, axis=0)
            return _online_softmax_step(s, v_ref[0, pl.ds(k0, tk), sl], carry)

        _, l, acc = lax.fori_loop(0, seq // tk, body, _softmax_init(4 * tq))
        p = acc / l
        head_a = p[0:tq] - lam * p[tq:2 * tq]
        head_b = p[2 * tq:3 * tq] - lam * p[3 * tq:4 * tq]
        o = jnp.where(lo, head_a, head_b)
        o = o * _pair_rms_scale(o) * subln_ref[...] * (1.0 - lambda_init)
        o_ref[0, :, sl] = o.astype(BF16)


def _diff_attention(qkv, lq1, lk1, lq2, lk2, subln, batch, seq, lambda_init):
    tq = DIFF_TQ
    return pl.pallas_call(
        functools.partial(_diff_kernel, seq=seq, lambda_init=lambda_init),
        grid=(batch, seq // tq),
        in_specs=[
            pl.BlockSpec((1, tq, 256), lambda b, t: (b, t, C_Q // 256)),
            pl.BlockSpec((1, seq, 256), lambda b, t: (b, 0, C_K // 256)),
            pl.BlockSpec((1, seq, 256), lambda b, t: (b, 0, C_V // 256)),
            _whole(), _whole(), _whole(), _whole(), _whole(),
        ],
        out_specs=pl.BlockSpec((1, tq, BRANCH_W), lambda b, t: (b, t, 0)),
        out_shape=jax.ShapeDtypeStruct((batch, seq, BRANCH_W), BF16),
        compiler_params=_params(2),
    )(qkv, qkv, qkv, lq1, lk1, lq2, lk2, subln)


def _swa_kernel(q_ref, k_ref, v_ref, sink_ref, o_ref, *, seq):
    tq = SWA_TQ
    span = tq + 2 * SWA_WINDOW
    q_start = pl.program_id(1) * tq
    k_start = pl.multiple_of(jnp.clip(q_start - SWA_WINDOW, 0, seq - span), SWA_WINDOW)
    q = q_ref[0]
    kw = k_ref[0, pl.ds(k_start, span), :]
    vw = v_ref[0, pl.ds(k_start, span), :]
    q_pos = q_start + lax.broadcasted_iota(jnp.int32, (tq, span), 0)
    k_pos = k_start + lax.broadcasted_iota(jnp.int32, (tq, span), 1)
    rel = jnp.abs(k_pos - q_pos)
    valid = rel <= SWA_WINDOW
    rel_f = rel.astype(F32)
    lo = lax.broadcasted_iota(jnp.int32, (tq, LANES), 1) < HEAD_DIM
    outs = [[None, None], [None, None]]
    for kv_head in range(2):
        s_all = _nt_dot(_stack_group(q, lo if kv_head == 0 else jnp.logical_not(lo)), kw)
        for g in range(2):
            head = 2 * kv_head + g
            sink = sink_ref[head]
            s = s_all[g * tq:(g + 1) * tq] - SWA_SLOPES[head] * rel_f
            s = jnp.where(valid, s, NEG_INF)
            m = jnp.maximum(jnp.max(s, axis=-1, keepdims=True), sink)
            e = jnp.exp(s - m)
            l = jnp.sum(e, axis=-1, keepdims=True) + jnp.exp(sink - m)
            outs[g][kv_head] = jnp.dot(e.astype(BF16), vw, preferred_element_type=F32) / l
    o_ref[0, :, :LANES] = jnp.where(lo, outs[0][0], outs[0][1]).astype(BF16)
    o_ref[0, :, LANES:] = jnp.where(lo, outs[1][0], outs[1][1]).astype(BF16)


def _swa_attention(qkv, sink, batch, seq):
    tq = SWA_TQ
    return pl.pallas_call(
        functools.partial(_swa_kernel, seq=seq),
        grid=(batch, seq // tq),
        in_specs=[
            pl.BlockSpec((1, tq, 256), lambda b, t: (b, t, D_Q // 256)),
            pl.BlockSpec((1, seq, LANES), lambda b, t: (b, 0, D_K // LANES)),
            pl.BlockSpec((1, seq, LANES), lambda b, t: (b, 0, D_V // LANES)),
            _whole(pltpu.SMEM),
        ],
        out_specs=pl.BlockSpec((1, tq, BRANCH_W), lambda b, t: (b, t, 0)),
        out_shape=jax.ShapeDtypeStruct((batch, seq, BRANCH_W), BF16),
        compiler_params=_params(2),
    )(qkv, qkv, qkv, sink)


def _merge_kernel(x_ref, oa_ref, ob_ref, oc_ref, od_ref, gpre_ref, wg_ref, wb_ref, wo_ref, gpost_ref, o_ref):
    x = x_ref[...]
    h = _rmsnorm(x, gpre_ref[...]).astype(BF16)
    merged = None
    for i, br_ref in enumerate((oa_ref, ob_ref, oc_ref, od_ref)):
        gate = jax.nn.sigmoid(jnp.dot(h, wg_ref[:, i * D_MODEL:(i + 1) * D_MODEL], preferred_element_type=F32))
        term = gate * jnp.dot(br_ref[...], wb_ref[i], preferred_element_type=F32)
        merged = term if merged is None else merged + term
    mix = jnp.dot(merged.astype(BF16), wo_ref[...], preferred_element_type=F32)
    o_ref[...] = x + _rmsnorm(mix, gpost_ref[...])


def _merge(x2d, branches, gpre, w_gate, w_branch, w_out, gpost):
    n_tok = x2d.shape[0]
    t = TOKEN_TILE
    tok = lambda w: pl.BlockSpec((t, w), lambda i: (i, 0))
    return pl.pallas_call(
        _merge_kernel,
        grid=(n_tok // t,),
        in_specs=[tok(D_MODEL)] + [tok(BRANCH_W)] * 4 + [_whole()] * 5,
        out_specs=tok(D_MODEL),
        out_shape=jax.ShapeDtypeStruct((n_tok, D_MODEL), F32),
        compiler_params=_params(1),
    )(x2d, *branches, gpre, w_gate, w_branch, w_out, gpost)


def _ffn_kernel(xp_ref, x_ref, xn_ref, gpre_ref, wup_ref, cw_ref, cb_ref, wdn_ref, gpost_ref, o_ref,
                h_scr, ug_scr, uv_scr, *, tiles_per_seq):
    t = TOKEN_TILE
    pos = pl.program_id(0) % tiles_per_seq
    gpre = gpre_ref[...]
    x = x_ref[...]
    keep_prev = (pos != 0).astype(F32)
    keep_next = (pos != tiles_per_seq - 1).astype(F32)
    h_scr[0:HALO, :] = (_rmsnorm(xp_ref[...], gpre) * keep_prev).astype(BF16)
    h_scr[HALO:HALO + t, :] = _rmsnorm(x, gpre).astype(BF16)
    h_scr[HALO + t:, :] = (_rmsnorm(xn_ref[...], gpre) * keep_next).astype(BF16)
    h = h_scr[...]

    def conv(u_scr, c0):
        u_scr[...] = jnp.dot(h, wup_ref[:, c0:c0 + FFN_CHUNK], preferred_element_type=F32)
        c = cb_ref[:, c0:c0 + FFN_CHUNK]
        for j in range(3):
            c = c + u_scr[HALO - 1 + j:HALO - 1 + j + t, :] * cw_ref[j:j + 1, c0:c0 + FFN_CHUNK]
        return c

    acc = jnp.zeros((t, D_MODEL), F32)
    for ch in range(D_FF // FFN_CHUNK):
        gate = conv(ug_scr, ch * FFN_CHUNK)
        val = conv(uv_scr, D_FF + ch * FFN_CHUNK)
        a = (jax.nn.gelu(gate, approximate=True) * val).astype(BF16)
        acc = acc + jnp.dot(a, wdn_ref[ch * FFN_CHUNK:(ch + 1) * FFN_CHUNK, :], preferred_element_type=F32)
    o_ref[...] = x + _rmsnorm(acc, gpost_ref[...])


def _ffn(x2d, gpre, w_up, conv_w, conv_b, w_down, gpost, seq):
    n_tok = x2d.shape[0]
    t = TOKEN_TILE
    per_tile = t // HALO
    n_halo_blocks = n_tok // HALO
    return pl.pallas_call(
        functools.partial(_ffn_kernel, tiles_per_seq=seq // t),
        grid=(n_tok // t,),
        in_specs=[
            pl.BlockSpec((HALO, D_MODEL), lambda i: (jnp.maximum(i * per_tile - 1, 0), 0)),
            pl.BlockSpec((t, D_MODEL), lambda i: (i, 0)),
            pl.BlockSpec((HALO, D_MODEL), lambda i: (jnp.minimum((i + 1) * per_tile, n_halo_blocks - 1), 0)),
        ] + [_whole()] * 6,
        out_specs=pl.BlockSpec((t, D_MODEL), lambda i: (i, 0)),
        out_shape=jax.ShapeDtypeStruct((n_tok, D_MODEL), F32),
        scratch_shapes=[
            pltpu.VMEM((t + 2 * HALO, D_MODEL), BF16),
            pltpu.VMEM((t + 2 * HALO, FFN_CHUNK), F32),
            pltpu.VMEM((t + 2 * HALO, FFN_CHUNK), F32),
        ],
        compiler_params=_params(1),
    )(x2d, x2d, x2d, gpre, w_up, conv_w, conv_b, w_down, gpost)


def _rope_tables(seq):
    t = np.arange(seq)
    axis_dim = HEAD_DIM // 2
    inv = jnp.asarray(ROPE_THETA, F32) ** (-jnp.arange(0, axis_dim, 2, dtype=F32) / axis_dim)
    ang_r = jnp.asarray(t // GRID_W, F32)[:, None] * inv
    ang_c = jnp.asarray(t % GRID_W, F32)[:, None] * inv
    cos = jnp.concatenate([jnp.cos(ang_r)] * 2 + [jnp.cos(ang_c)] * 2, axis=-1)
    sin = jnp.concatenate([-jnp.sin(ang_r), jnp.sin(ang_r), -jnp.sin(ang_c), jnp.sin(ang_c)], axis=-1)
    return jnp.tile(cos, (1, 2)), jnp.tile(sin, (1, 2))


def _group_major(n_heads=4):
    order = [0, 2, 1, 3]
    return np.concatenate([np.arange(h * HEAD_DIM, (h + 1) * HEAD_DIM) for h in order])


def kernel(x_prompt, x_sample, norm_mix_pre, norm_mix_post, norm_ffn_pre, norm_ffn_post, w_in, na_rpb, gqa_q_norm, gqa_k_norm, diff_lambda_q1, diff_lambda_k1, diff_lambda_q2, diff_lambda_k2, diff_subln, swa_sink, w_branch, w_out, ffn_w_up, ffn_conv_w, ffn_conv_b, ffn_w_down):
    depth = w_in.shape[0]
    gm = _group_major()
    col_perm = np.arange(QKV_COLS)
    col_perm[B_Q:B_Q + 256] = B_Q + gm
    col_perm[D_Q:D_Q + 256] = D_Q + gm
    row_perm = np.arange(N_BRANCH * BRANCH_W)
    row_perm[BRANCH_W:2 * BRANCH_W] = BRANCH_W + gm
    row_perm[3 * BRANCH_W:] = 3 * BRANCH_W + gm

    w_qkv = w_in[:, :, :QKV_COLS][:, :, col_perm].astype(BF16)
    w_gate = w_in[:, :, QKV_COLS:].astype(BF16)
    w_br = w_branch[:, row_perm, :].reshape(depth, N_BRANCH, BRANCH_W, D_MODEL).astype(BF16)
    w_o = w_out.astype(BF16)
    w_up = ffn_w_up.astype(BF16)
    w_dn = ffn_w_down.astype(BF16)
    row2 = lambda a: a.astype(F32).reshape(depth, 1, -1)
    tile2 = lambda a: jnp.tile(a.astype(F32), (1, 2)).reshape(depth, 1, -1)
    g_mix_pre, g_mix_post = row2(norm_mix_pre), row2(norm_mix_post)
    g_ffn_pre, g_ffn_post = row2(norm_ffn_pre), row2(norm_ffn_post)
    qn, kn, subln = tile2(gqa_q_norm), tile2(gqa_k_norm), tile2(diff_subln)
    lq1, lk1, lq2, lk2 = (row2(a) for a in (diff_lambda_q1, diff_lambda_k1, diff_lambda_q2, diff_lambda_k2))
    conv_b = row2(ffn_conv_b)
    conv_w = ffn_conv_w.astype(F32)
    sink = swa_sink.astype(F32)

    def run_trunk(x):
        batch, seq, _ = x.shape
        cos_t, sin_t = _rope_tables(seq)
        na_bias = _na_bias_tables(na_rpb.astype(F32), seq // GRID_W)
        x2d = x.reshape(batch * seq, D_MODEL)
        for l in range(depth):
            lambda_init = 0.8 - 0.6 * math.exp(-0.3 * l)
            qkv = _in_proj(x2d, g_mix_pre[l], w_qkv[l], cos_t, sin_t, qn[l], kn[l], seq)
            qkv = qkv.reshape(batch, seq, QKV_COLS)
            o_a = _na_attention(qkv, na_bias[l], batch, seq)
            o_b = _gqa_attention(qkv, batch, seq)
            o_c = _diff_attention(qkv, lq1[l], lk1[l], lq2[l], lk2[l], subln[l], batch, seq, lambda_init)
            o_d = _swa_attention(qkv, sink[l], batch, seq)
            branches = [o.reshape(batch * seq, BRANCH_W) for o in (o_a, o_b, o_c, o_d)]
            x2d = _merge(x2d, branches, g_mix_pre[l], w_gate[l], w_br[l], w_o[l], g_mix_post[l])
            x2d = _ffn(x2d, g_ffn_pre[l], w_up[l], conv_w[l], conv_b[l], w_dn[l], g_ffn_post[l], seq)
        return x2d.reshape(batch, seq, D_MODEL)

    return (run_trunk(x_prompt), run_trunk(x_sample))
```

```python
import functools
import math

import numpy as np
import jax
import jax.numpy as jnp
from jax import lax
from jax.experimental import pallas as pl
from jax.experimental.pallas import tpu as pltpu

F32 = jnp.float32
BF16 = jnp.bfloat16

D_MODEL = 1024
GRID_W = 64
HEAD_DIM = 64
EPS = 1e-6
NEG_INF = -1e30
LOG2E = math.log2(math.e)
NA_WIN_R = 8
NA_WIN_C = 16
ROPE_THETA = 10000.0
DIFF_QK_DIM = 32
SWA_WINDOW = 128
N_ALIBI_HEADS = 8
N_BRANCH = 4
BRANCH_W = 256
D_FF = 2816
QKV_COLS = 2560

A_Q, A_K, A_V = 0, 256, 512
B_Q, B_K, B_V = 768, 1024, 1152
C_Q, C_K, C_V = 1280, 1536, 1792
D_Q, D_K, D_V = 2048, 2304, 2432

LANES = 128
VMEM_LIMIT = 56 * 1024 * 1024

TOKEN_TILE = 512
HALO = 16
FFN_CHUNK = 256
NA_Q_ROWS = 4
NA_K_ROWS = 12
GQA_TQ, GQA_TK = 512, 1024
DIFF_TQ, DIFF_TK = 256, 1024
SWA_TQ = 256


def _alibi_slopes():
    s = 2.0 ** (-8.0 * np.arange(1, N_ALIBI_HEADS + 1) / N_ALIBI_HEADS)
    return [float(v) for v in s[0::2]], [float(v) for v in s[1::2]]


DIFF_SLOPES, SWA_SLOPES = _alibi_slopes()


def _params(n_parallel):
    return pltpu.CompilerParams(dimension_semantics=("parallel",) * n_parallel,
                                vmem_limit_bytes=VMEM_LIMIT)


def _whole(space=pltpu.VMEM):
    return pl.BlockSpec(memory_space=space)


def _rmsnorm(x, gain):
    ms = jnp.mean(x * x, axis=-1, keepdims=True)
    return x * lax.rsqrt(ms + EPS) * gain


def _pair_rms_scale(y):
    lo = lax.broadcasted_iota(jnp.int32, y.shape, 1) < HEAD_DIM
    y2 = y * y
    s_lo = jnp.sum(jnp.where(lo, y2, 0.0), axis=-1, keepdims=True)
    s_hi = jnp.sum(jnp.where(lo, 0.0, y2), axis=-1, keepdims=True)
    ms = jnp.where(lo, s_lo, s_hi) * (1.0 / HEAD_DIM)
    return lax.rsqrt(ms + EPS)


def _nt_dot(a, b):
    return lax.dot_general(a, b, (((1,), (1,)), ((), ())), preferred_element_type=F32)


def _lo_lanes(rows):
    return lax.broadcasted_iota(jnp.int32, (rows, LANES), 1) < HEAD_DIM


def _own(lo, half):
    return lo if half == 0 else jnp.logical_not(lo)


def _pv_with_rowsum(p, v, own):
    return jnp.dot(p.astype(BF16), jnp.where(own, v, jnp.ones_like(v)), preferred_element_type=F32)


def _normalise(acc):
    return acc / pltpu.roll(acc, HEAD_DIM, 1)


def _in_proj_kernel(x_ref, gain_ref, w_ref, cos_ref, sin_ref, qn_ref, kn_ref, o_ref):
    h = _rmsnorm(x_ref[...], gain_ref[...]).astype(BF16)

    def proj(c0, width):
        return jnp.dot(h, w_ref[:, c0:c0 + width], preferred_element_type=F32)

    def plain(c0, width, scale=None):
        y = proj(c0, width)
        if scale is not None:
            y = y * scale
        o_ref[:, c0:c0 + width] = y.astype(BF16)

    def normed_rope(c0, gain_ref_, scale=None):
        y = proj(c0, LANES)
        yn = y * _pair_rms_scale(y) * gain_ref_[...]
        lane = lax.broadcasted_iota(jnp.int32, yn.shape, 1)
        partner = jnp.where((lane % 32) < 16, pltpu.roll(yn, LANES - 16, 1), pltpu.roll(yn, 16, 1))
        out = yn * cos_ref[...] + partner * sin_ref[...]
        if scale is not None:
            out = out * scale
        o_ref[:, c0:c0 + LANES] = out.astype(BF16)

    hd_scale = HEAD_DIM ** -0.5 * LOG2E
    plain(A_Q, 256, hd_scale)
    plain(A_K, 512)
    normed_rope(B_Q, qn_ref, hd_scale)
    normed_rope(B_Q + LANES, qn_ref, hd_scale)
    normed_rope(B_K, kn_ref)
    plain(B_V, 128)
    plain(C_Q, 256, DIFF_QK_DIM ** -0.5 * LOG2E)
    plain(C_K, 512)
    plain(D_Q, 256, hd_scale)
    plain(D_K, 256)


def _in_proj(x2d, gain, w_qkv, cos_t, sin_t, qn, kn, seq):
    n_tok = x2d.shape[0]
    t = TOKEN_TILE
    tiles_per_seq = seq // t
    return pl.pallas_call(
        _in_proj_kernel,
        grid=(n_tok // t,),
        in_specs=[
            pl.BlockSpec((t, D_MODEL), lambda i: (i, 0)),
            _whole(), _whole(),
            pl.BlockSpec((t, LANES), lambda i: (i % tiles_per_seq, 0)),
            pl.BlockSpec((t, LANES), lambda i: (i % tiles_per_seq, 0)),
            _whole(), _whole(),
        ],
        out_specs=pl.BlockSpec((t, QKV_COLS), lambda i: (i, 0)),
        out_shape=jax.ShapeDtypeStruct((n_tok, QKV_COLS), BF16),
        compiler_params=_params(1),
        name="in_proj",
    )(x2d, gain, w_qkv, cos_t, sin_t, qn, kn)


def _na_kernel(q_ref, k_ref, v_ref, b_ref, o_ref, *, rows):
    t = pl.program_id(1)
    tq = NA_Q_ROWS * GRID_W
    tk = NA_K_ROWS * GRID_W
    r_start = jnp.clip(t * NA_Q_ROWS - NA_WIN_R // 2, 0, rows - NA_K_ROWS)
    k0 = pl.multiple_of(r_start * GRID_W, GRID_W)
    q = q_ref[0]
    kw = k_ref[0, pl.ds(k0, tk), :]
    vw = v_ref[0, pl.ds(k0, tk), :]
    lo_q, lo_k = _lo_lanes(tq), _lo_lanes(tk)
    for pair in range(2):
        sl = slice(pair * LANES, (pair + 1) * LANES)
        qp, kp, vp = q[:, sl], kw[:, sl], vw[:, sl]
        halves = []
        for half in range(2):
            qm = jnp.where(_own(lo_q, half), qp, jnp.zeros_like(qp))
            s = _nt_dot(qm, kp) + b_ref[0, 2 * pair + half]
            e = jnp.exp2(s - jnp.max(s, axis=-1, keepdims=True))
            halves.append(_normalise(_pv_with_rowsum(e, vp, _own(lo_k, half))))
        o_ref[0, :, sl] = jnp.where(lo_q, halves[0], halves[1]).astype(BF16)


def _na_bias_tables(rpb, rows):
    depth, heads = rpb.shape[0], rpb.shape[1]
    qc, kc = np.arange(GRID_W)[:, None], np.arange(GRID_W)[None, :]
    c0 = np.clip(qc - NA_WIN_C // 2, 0, GRID_W - NA_WIN_C)
    col_valid = (kc >= c0) & (kc < c0 + NA_WIN_C)
    dc = np.clip(kc - qc + NA_WIN_C - 1, 0, 2 * NA_WIN_C - 2)
    blocks = jnp.where(col_valid, rpb[:, :, :, dc] * LOG2E, NEG_INF).astype(F32)
    masked = jnp.full((depth, heads, GRID_W, GRID_W), NEG_INF, F32)
    cases = []
    for q_row0 in (0, NA_Q_ROWS, rows - NA_Q_ROWS):
        r_start = int(np.clip(q_row0 - NA_WIN_R // 2, 0, rows - NA_K_ROWS))
        strips = []
        for i in range(NA_Q_ROWS):
            qr = q_row0 + i
            r0 = int(np.clip(qr - NA_WIN_R // 2, 0, rows - NA_WIN_R))
            strip = []
            for j in range(NA_K_ROWS):
                kr = r_start + j
                strip.append(blocks[:, :, kr - qr + NA_WIN_R - 1] if r0 <= kr < r0 + NA_WIN_R else masked)
            strips.append(jnp.concatenate(strip, axis=-1))
        cases.append(jnp.concatenate(strips, axis=-2))
    return jnp.stack(cases, axis=1)


def _na_attention(qkv, bias, batch, seq):
    rows = seq // GRID_W
    n_t = rows // NA_Q_ROWS
    tq, tk = NA_Q_ROWS * GRID_W, NA_K_ROWS * GRID_W
    case = lambda t: jnp.where(t == 0, 0, jnp.where(t == n_t - 1, 2, 1))
    return pl.pallas_call(
        functools.partial(_na_kernel, rows=rows),
        grid=(batch, n_t),
        in_specs=[
            pl.BlockSpec((1, tq, 256), lambda b, t: (b, t, A_Q // 256)),
            pl.BlockSpec((1, seq, 256), lambda b, t: (b, 0, A_K // 256)),
            pl.BlockSpec((1, seq, 256), lambda b, t: (b, 0, A_V // 256)),
            pl.BlockSpec((1, 4, tq, tk), lambda b, t: (case(t), 0, 0, 0)),
        ],
        out_specs=pl.BlockSpec((1, tq, BRANCH_W), lambda b, t: (b, t, 0)),
        out_shape=jax.ShapeDtypeStruct((batch, seq, BRANCH_W), BF16),
        compiler_params=_params(2),
        name="na_attention",
    )(qkv, qkv, qkv, bias)


def _online_softmax_step(s, pv_fn, carry):
    m, acc = carry
    m_new = jnp.maximum(m, jnp.max(s, axis=-1, keepdims=True))
    alpha = jnp.exp2(m - m_new)
    return m_new, alpha * acc + pv_fn(jnp.exp2(s - m_new))


def _softmax_init(rows):
    return (jnp.full((rows, 1), NEG_INF, F32), jnp.zeros((rows, LANES), F32))


def _stack_group(q, sel):
    g0, g1 = q[:, :LANES], q[:, LANES:]
    zero = jnp.zeros_like(g0)
    return jnp.concatenate([jnp.where(sel, g0, zero), jnp.where(sel, g1, zero)], axis=0)


def _gqa_kernel(q_ref, k_ref, v_ref, o_ref, *, seq):
    tq, tk = GQA_TQ, GQA_TK
    q = q_ref[0]
    lo_q, lo_k = _lo_lanes(tq), _lo_lanes(tk)
    qq = [_stack_group(q, _own(lo_q, h)) for h in range(2)]

    def body(j, carry):
        k0 = pl.multiple_of(j * tk, tk)
        kj = k_ref[0, pl.ds(k0, tk), :]
        vj = v_ref[0, pl.ds(k0, tk), :]
        return tuple(
            _online_softmax_step(_nt_dot(qq[h], kj),
                                 functools.partial(_pv_with_rowsum, v=vj, own=_own(lo_k, h)), carry[h])
            for h in range(2))

    res = lax.fori_loop(0, seq // tk, body, (_softmax_init(2 * tq), _softmax_init(2 * tq)), unroll=2)
    n = [_normalise(acc) for _, acc in res]
    o_ref[0, :, :LANES] = jnp.where(lo_q, n[0][:tq], n[1][:tq]).astype(BF16)
    o_ref[0, :, LANES:] = jnp.where(lo_q, n[0][tq:], n[1][tq:]).astype(BF16)


def _gqa_attention(qkv, batch, seq):
    tq = GQA_TQ
    return pl.pallas_call(
        functools.partial(_gqa_kernel, seq=seq),
        grid=(batch, seq // tq),
        in_specs=[
            pl.BlockSpec((1, tq, 256), lambda b, t: (b, t, B_Q // 256)),
            pl.BlockSpec((1, seq, LANES), lambda b, t: (b, 0, B_K // LANES)),
            pl.BlockSpec((1, seq, LANES), lambda b, t: (b, 0, B_V // LANES)),
        ],
        out_specs=pl.BlockSpec((1, tq, BRANCH_W), lambda b, t: (b, t, 0)),
        out_shape=jax.ShapeDtypeStruct((batch, seq, BRANCH_W), BF16),
        compiler_params=_params(2),
        name="gqa_attention",
    )(qkv, qkv, qkv)


_AUG_KEY_LO, _AUG_KEY_HI, _AUG_ROW, _AUG_DELTA = 0, 3, 6, 9
_AUG_MASK = 12


def _bf16_pieces(x):
    x = np.asarray(x, np.float32)
    pieces = []
    for _ in range(3):
        p = x.astype(BF16).astype(np.float32)
        pieces.append(p)
        x = x - p
    return pieces


def _diff_bias_factors():
    tq, tk = DIFF_TQ, DIFF_TK
    q_side = np.zeros((4, tq, LANES), np.float32)
    for h, slope in enumerate(DIFF_SLOPES):
        c = np.float32(slope * LOG2E)
        c3 = _bf16_pieces(c)
        r3 = _bf16_pieces(-c * np.arange(tq, dtype=np.float32))
        for n in range(3):
            for g in (_AUG_KEY_LO, _AUG_KEY_HI, _AUG_DELTA):
                q_side[h, :, g + n] = c3[n]
            q_side[h, :, _AUG_ROW + n] = r3[n]
        q_side[h, :, _AUG_MASK] = 1.0
    k_side = np.zeros((tk, LANES), np.float32)
    j = np.arange(tk)
    for n in range(3):
        k_side[:, _AUG_KEY_LO + n] = j % 256
        k_side[:, _AUG_KEY_HI + n] = 256 * (j // 256)
        k_side[:, _AUG_ROW + n] = 1.0
    return jnp.asarray(q_side, BF16), jnp.asarray(k_side, F32)


def _diff_kernel(q_ref, k_ref, v_ref, qf_ref, kf_ref, lq1_ref, lk1_ref, lq2_ref, lk2_ref, subln_ref, o_ref, *,
                 seq, lambda_init):
    tq, tk = DIFF_TQ, DIFF_TK
    q_start = pl.multiple_of(pl.program_id(1) * tq, tq)
    q = q_ref[0]
    lam = (jnp.exp(jnp.sum(lq1_ref[...] * lk1_ref[...], axis=-1, keepdims=True))
           - jnp.exp(jnp.sum(lq2_ref[...] * lk2_ref[...], axis=-1, keepdims=True)) + lambda_init)
    lane = lax.broadcasted_iota(jnp.int32, (tq, LANES), 1)
    seg = lane // DIFF_QK_DIM
    lo_q, lo_k = lane < HEAD_DIM, _lo_lanes(tk)
    lane_k = lax.broadcasted_iota(jnp.int32, (tk, LANES), 1)
    key_k = lax.broadcasted_iota(jnp.int32, (tk, LANES), 0)
    delta_lanes = (lane_k >= _AUG_DELTA) & (lane_k < _AUG_DELTA + 3)
    mask_lane = lane_k == _AUG_MASK
    own_dist = jnp.abs(lax.broadcasted_iota(jnp.int32, (tq, tq), 0)
                       - lax.broadcasted_iota(jnp.int32, (tq, tq), 1)).astype(F32)

    def pv(p, v, lo):
        return jnp.concatenate([_pv_with_rowsum(p[:2 * tq], v, lo),
                                _pv_with_rowsum(p[2 * tq:], v, jnp.logical_not(lo))], axis=0)

    qq_aug, carry = [], []
    for pair in range(2):
        sl = slice(pair * LANES, (pair + 1) * LANES)
        qp = q[:, sl]
        zero = jnp.zeros_like(qp)
        rows4 = jnp.concatenate([jnp.where(seg == i, qp, zero) for i in range(4)], axis=0)
        factors = jnp.concatenate([qf_ref[2 * pair]] * 2 + [qf_ref[2 * pair + 1]] * 2, axis=0)
        qq_aug.append(jnp.concatenate([rows4, factors], axis=1))
        bias = jnp.concatenate([(DIFF_SLOPES[2 * pair] * LOG2E) * own_dist] * 2
                               + [(DIFF_SLOPES[2 * pair + 1] * LOG2E) * own_dist] * 2, axis=0)
        s = _nt_dot(rows4, k_ref[0, pl.ds(q_start, tq), sl]) - bias
        m = jnp.max(s, axis=-1, keepdims=True)
        carry.append((m, pv(jnp.exp2(s - m), v_ref[0, pl.ds(q_start, tq), sl], lo_q)))

    def body(j, carry):
        k0 = pl.multiple_of(j * tk, tk)
        rel = key_k + (k0 - q_start)
        sign = jnp.where(rel < 0, 1.0, -1.0)
        kf = sign * (kf_ref[...] - jnp.where(delta_lanes, (q_start - k0).astype(F32), 0.0))
        own_key = jnp.where(rel >= 0, jnp.where(rel < tq, NEG_INF, 0.0), 0.0)
        k_factors = jnp.where(mask_lane, own_key, kf).astype(BF16)
        new = []
        for pair in range(2):
            sl = slice(pair * LANES, (pair + 1) * LANES)
            s = _nt_dot(qq_aug[pair], jnp.concatenate([k_ref[0, pl.ds(k0, tk), sl], k_factors], axis=1))
            new.append(_online_softmax_step(
                s, functools.partial(pv, v=v_ref[0, pl.ds(k0, tk), sl], lo=lo_k), carry[pair]))
        return tuple(new)

    carry = lax.fori_loop(0, seq // tk, body, tuple(carry), unroll=2)
    for pair in range(2):
        p = _normalise(carry[pair][1])
        head_a = p[0:tq] - lam * p[tq:2 * tq]
        head_b = p[2 * tq:3 * tq] - lam * p[3 * tq:4 * tq]
        o = jnp.where(lo_q, head_a, head_b)
        o = o * _pair_rms_scale(o) * subln_ref[...] * (1.0 - lambda_init)
        o_ref[0, :, pair * LANES:(pair + 1) * LANES] = o.astype(BF16)


def _diff_attention(qkv, q_factors, k_factors, lq1, lk1, lq2, lk2, subln, batch, seq, lambda_init):
    tq = DIFF_TQ
    return pl.pallas_call(
        functools.partial(_diff_kernel, seq=seq, lambda_init=lambda_init),
        grid=(batch, seq // tq),
        in_specs=[
            pl.BlockSpec((1, tq, 256), lambda b, t: (b, t, C_Q // 256)),
            pl.BlockSpec((1, seq, 256), lambda b, t: (b, 0, C_K // 256)),
            pl.BlockSpec((1, seq, 256), lambda b, t: (b, 0, C_V // 256)),
        ] + [_whole()] * 7,
        out_specs=pl.BlockSpec((1, tq, BRANCH_W), lambda b, t: (b, t, 0)),
        out_shape=jax.ShapeDtypeStruct((batch, seq, BRANCH_W), BF16),
        compiler_params=_params(2),
        name="diff_attention",
    )(qkv, qkv, qkv, q_factors, k_factors, lq1, lk1, lq2, lk2, subln)


def _swa_kernel(q_ref, k_ref, v_ref, sink_ref, o_ref, *, seq):
    tq = SWA_TQ
    span = tq + 2 * SWA_WINDOW
    q_start = pl.program_id(1) * tq
    k_start = pl.multiple_of(jnp.clip(q_start - SWA_WINDOW, 0, seq - span), SWA_WINDOW)
    q = q_ref[0]
    kw = k_ref[0, pl.ds(k_start, span), :]
    vw = v_ref[0, pl.ds(k_start, span), :]
    q_pos = q_start + lax.broadcasted_iota(jnp.int32, (tq, span), 0)
    k_pos = k_start + lax.broadcasted_iota(jnp.int32, (tq, span), 1)
    rel = jnp.abs(k_pos - q_pos)
    valid = rel <= SWA_WINDOW
    rel_f = rel.astype(F32)
    lo_q, lo_k = _lo_lanes(tq), _lo_lanes(span)
    outs = [[None, None], [None, None]]
    for kv_head in range(2):
        s_all = _nt_dot(_stack_group(q, _own(lo_q, kv_head)), kw)
        for g in range(2):
            head = 2 * kv_head + g
            sink = sink_ref[head] * LOG2E
            s = s_all[g * tq:(g + 1) * tq] - (SWA_SLOPES[head] * LOG2E) * rel_f
            s = jnp.where(valid, s, NEG_INF)
            m = jnp.maximum(jnp.max(s, axis=-1, keepdims=True), sink)
            r = _pv_with_rowsum(jnp.exp2(s - m), vw, _own(lo_k, kv_head))
            outs[g][kv_head] = r / (pltpu.roll(r, HEAD_DIM, 1) + jnp.exp2(sink - m))
    o_ref[0, :, :LANES] = jnp.where(lo_q, outs[0][0], outs[0][1]).astype(BF16)
    o_ref[0, :, LANES:] = jnp.where(lo_q, outs[1][0], outs[1][1]).astype(BF16)


def _swa_attention(qkv, sink, batch, seq):
    tq = SWA_TQ
    return pl.pallas_call(
        functools.partial(_swa_kernel, seq=seq),
        grid=(batch, seq // tq),
        in_specs=[
            pl.BlockSpec((1, tq, 256), lambda b, t: (b, t, D_Q // 256)),
            pl.BlockSpec((1, seq, LANES), lambda b, t: (b, 0, D_K // LANES)),
            pl.BlockSpec((1, seq, LANES), lambda b, t: (b, 0, D_V // LANES)),
            _whole(pltpu.SMEM),
        ],
        out_specs=pl.BlockSpec((1, tq, BRANCH_W), lambda b, t: (b, t, 0)),
        out_shape=jax.ShapeDtypeStruct((batch, seq, BRANCH_W), BF16),
        compiler_params=_params(2),
        name="swa_attention",
    )(qkv, qkv, qkv, sink)


def _merge_kernel(x_ref, oa_ref, ob_ref, oc_ref, od_ref, gpre_ref, wg_ref, wb_ref, wo_ref, gpost_ref, o_ref):
    x = x_ref[...]
    h = _rmsnorm(x, gpre_ref[...]).astype(BF16)
    merged = None
    for i, br_ref in enumerate((oa_ref, ob_ref, oc_ref, od_ref)):
        gate = jax.nn.sigmoid(jnp.dot(h, wg_ref[:, i * D_MODEL:(i + 1) * D_MODEL], preferred_element_type=F32))
        term = gate * jnp.dot(br_ref[...], wb_ref[i], preferred_element_type=F32)
        merged = term if merged is None else merged + term
    mix = jnp.dot(merged.astype(BF16), wo_ref[...], preferred_element_type=F32)
    o_ref[...] = x + _rmsnorm(mix, gpost_ref[...])


def _merge(x2d, branches, gpre, w_gate, w_branch, w_out, gpost):
    n_tok = x2d.shape[0]
    t = TOKEN_TILE
    tok = lambda w: pl.BlockSpec((t, w), lambda i: (i, 0))
    return pl.pallas_call(
        _merge_kernel,
        grid=(n_tok // t,),
        in_specs=[tok(D_MODEL)] + [tok(BRANCH_W)] * 4 + [_whole()] * 5,
        out_specs=tok(D_MODEL),
        out_shape=jax.ShapeDtypeStruct((n_tok, D_MODEL), F32),
        compiler_params=_params(1),
        name="merge",
    )(x2d, *branches, gpre, w_gate, w_branch, w_out, gpost)


def _ffn_kernel(xp_ref, x_ref, xn_ref, gpre_ref, wup_ref, cw_ref, cb_ref, wdn_ref, gpost_ref, o_ref,
                h_scr, ug_scr, uv_scr, *, tiles_per_seq):
    t = TOKEN_TILE
    pos = pl.program_id(0) % tiles_per_seq
    gpre = gpre_ref[...]
    x = x_ref[...]
    keep_prev = (pos != 0).astype(F32)
    keep_next = (pos != tiles_per_seq - 1).astype(F32)
    h_scr[0:HALO, :] = (_rmsnorm(xp_ref[...], gpre) * keep_prev).astype(BF16)
    h_scr[HALO:HALO + t, :] = _rmsnorm(x, gpre).astype(BF16)
    h_scr[HALO + t:, :] = (_rmsnorm(xn_ref[...], gpre) * keep_next).astype(BF16)
    h = h_scr[...]

    def conv(u_scr, c0):
        u_scr[...] = jnp.dot(h, wup_ref[:, c0:c0 + FFN_CHUNK], preferred_element_type=F32)
        c = cb_ref[:, c0:c0 + FFN_CHUNK]
        for j in range(3):
            c = c + u_scr[HALO - 1 + j:HALO - 1 + j + t, :] * cw_ref[j:j + 1, c0:c0 + FFN_CHUNK]
        return c

    acc = jnp.zeros((t, D_MODEL), F32)
    for ch in range(D_FF // FFN_CHUNK):
        gate = conv(ug_scr, ch * FFN_CHUNK)
        val = conv(uv_scr, D_FF + ch * FFN_CHUNK)
        a = (jax.nn.gelu(gate, approximate=True) * val).astype(BF16)
        acc = acc + jnp.dot(a, wdn_ref[ch * FFN_CHUNK:(ch + 1) * FFN_CHUNK, :], preferred_element_type=F32)
    o_ref[...] = x + _rmsnorm(acc, gpost_ref[...])


def _ffn(x2d, gpre, w_up, conv_w, conv_b, w_down, gpost, seq):
    n_tok = x2d.shape[0]
    t = TOKEN_TILE
    per_tile = t // HALO
    n_halo_blocks = n_tok // HALO
    return pl.pallas_call(
        functools.partial(_ffn_kernel, tiles_per_seq=seq // t),
        grid=(n_tok // t,),
        in_specs=[
            pl.BlockSpec((HALO, D_MODEL), lambda i: (jnp.maximum(i * per_tile - 1, 0), 0)),
            pl.BlockSpec((t, D_MODEL), lambda i: (i, 0)),
            pl.BlockSpec((HALO, D_MODEL), lambda i: (jnp.minimum((i + 1) * per_tile, n_halo_blocks - 1), 0)),
        ] + [_whole()] * 6,
        out_specs=pl.BlockSpec((t, D_MODEL), lambda i: (i, 0)),
        out_shape=jax.ShapeDtypeStruct((n_tok, D_MODEL), F32),
        scratch_shapes=[
            pltpu.VMEM((t + 2 * HALO, D_MODEL), BF16),
            pltpu.VMEM((t + 2 * HALO, FFN_CHUNK), F32),
            pltpu.VMEM((t + 2 * HALO, FFN_CHUNK), F32),
        ],
        compiler_params=_params(1),
        name="conv_ffn",
    )(x2d, x2d, x2d, gpre, w_up, conv_w, conv_b, w_down, gpost)


def _rope_tables(seq):
    t = np.arange(seq)
    axis_dim = HEAD_DIM // 2
    inv = jnp.asarray(ROPE_THETA, F32) ** (-jnp.arange(0, axis_dim, 2, dtype=F32) / axis_dim)
    ang_r = jnp.asarray(t // GRID_W, F32)[:, None] * inv
    ang_c = jnp.asarray(t % GRID_W, F32)[:, None] * inv
    cos = jnp.concatenate([jnp.cos(ang_r)] * 2 + [jnp.cos(ang_c)] * 2, axis=-1)
    sin = jnp.concatenate([-jnp.sin(ang_r), jnp.sin(ang_r), -jnp.sin(ang_c), jnp.sin(ang_c)], axis=-1)
    return jnp.tile(cos, (1, 2)), jnp.tile(sin, (1, 2))


def _group_major(w, axis):
    shape = w.shape
    w = w.reshape(shape[:axis] + (2, 2, HEAD_DIM) + shape[axis + 1:])
    return jnp.swapaxes(w, axis, axis + 1).reshape(shape)


def kernel(x_prompt, x_sample, norm_mix_pre, norm_mix_post, norm_ffn_pre, norm_ffn_post, w_in, na_rpb, gqa_q_norm, gqa_k_norm, diff_lambda_q1, diff_lambda_k1, diff_lambda_q2, diff_lambda_k2, diff_subln, swa_sink, w_branch, w_out, ffn_w_up, ffn_conv_w, ffn_conv_b, ffn_w_down):
    depth = w_in.shape[0]
    w_qkv = jnp.concatenate([
        w_in[:, :, :B_Q], _group_major(w_in[:, :, B_Q:B_K], 2), w_in[:, :, B_K:D_Q],
        _group_major(w_in[:, :, D_Q:D_K], 2), w_in[:, :, D_K:QKV_COLS]], axis=-1).astype(BF16)
    w_gate = w_in[:, :, QKV_COLS:].astype(BF16)
    w_br = w_branch.reshape(depth, N_BRANCH, BRANCH_W, D_MODEL)
    w_br = jnp.stack([w_br[:, 0], _group_major(w_br[:, 1], 1), w_br[:, 2], _group_major(w_br[:, 3], 1)],
                     axis=1).astype(BF16)
    diff_qf, diff_kf = _diff_bias_factors()
    w_o = w_out.astype(BF16)
    w_up = ffn_w_up.astype(BF16)
    w_dn = ffn_w_down.astype(BF16)
    row2 = lambda a: a.astype(F32).reshape(depth, 1, -1)
    tile2 = lambda a: jnp.tile(a.astype(F32), (1, 2)).reshape(depth, 1, -1)
    g_mix_pre, g_mix_post = row2(norm_mix_pre), row2(norm_mix_post)
    g_ffn_pre, g_ffn_post = row2(norm_ffn_pre), row2(norm_ffn_post)
    qn, kn, subln = tile2(gqa_q_norm), tile2(gqa_k_norm), tile2(diff_subln)
    lq1, lk1, lq2, lk2 = (row2(a) for a in (diff_lambda_q1, diff_lambda_k1, diff_lambda_q2, diff_lambda_k2))
    conv_b = row2(ffn_conv_b)
    conv_w = ffn_conv_w.astype(F32)
    sink = swa_sink.astype(F32)

    def run_trunk(x):
        batch, seq, _ = x.shape
        cos_t, sin_t = _rope_tables(seq)
        na_bias = _na_bias_tables(na_rpb.astype(F32), seq // GRID_W)
        x2d = x.reshape(batch * seq, D_MODEL)
        for l in range(depth):
            lambda_init = 0.8 - 0.6 * math.exp(-0.3 * l)
            qkv = _in_proj(x2d, g_mix_pre[l], w_qkv[l], cos_t, sin_t, qn[l], kn[l], seq)
            qkv = qkv.reshape(batch, seq, QKV_COLS)
            o_a = _na_attention(qkv, na_bias[l], batch, seq)
            o_b = _gqa_attention(qkv, batch, seq)
            o_c = _diff_attention(qkv, diff_qf, diff_kf, lq1[l], lk1[l], lq2[l], lk2[l], subln[l], batch, seq,
                                  lambda_init)
            o_d = _swa_attention(qkv, sink[l], batch, seq)
            branches = [o.reshape(batch * seq, BRANCH_W) for o in (o_a, o_b, o_c, o_d)]
            x2d = _merge(x2d, branches, g_mix_pre[l], w_gate[l], w_br[l], w_o[l], g_mix_post[l])
            x2d = _ffn(x2d, g_ffn_pre[l], w_up[l], conv_w[l], conv_b[l], w_dn[l], g_ffn_post[l], seq)
        return x2d.reshape(batch, seq, D_MODEL)

    return (run_trunk(x_prompt), run_trunk(x_sample))
```

```python
import functools
import math

import numpy as np
import jax
import jax.numpy as jnp
from jax import lax
from jax.experimental import pallas as pl
from jax.experimental.pallas import tpu as pltpu

F32 = jnp.float32
BF16 = jnp.bfloat16

D_MODEL = 1024
GRID_W = 64
HEAD_DIM = 64
EPS = 1e-6
NEG_INF = -1e30
LOG2E = math.log2(math.e)
NA_WIN_R = 8
NA_WIN_C = 16
ROPE_THETA = 10000.0
DIFF_QK_DIM = 32
SWA_WINDOW = 128
N_ALIBI_HEADS = 8
N_BRANCH = 4
BRANCH_W = 256
D_FF = 2816
QKV_COLS = 2560

A_Q, A_K, A_V = 0, 256, 512
B_Q, B_K, B_V = 768, 1024, 1152
C_Q, C_K, C_V = 1280, 1536, 1792
D_Q, D_K, D_V = 2048, 2304, 2432

LANES = 128
VMEM_LIMIT = 56 * 1024 * 1024

TOKEN_TILE = 512
HALO = 16
FFN_CHUNK = 256
NA_Q_ROWS = 4
NA_K_ROWS = 12
GQA_TQ, GQA_TK = 512, 1024
DIFF_TQ, DIFF_TK = 256, 1024
SWA_TQ = 256


def _alibi_slopes():
    s = 2.0 ** (-8.0 * np.arange(1, N_ALIBI_HEADS + 1) / N_ALIBI_HEADS)
    return [float(v) for v in s[0::2]], [float(v) for v in s[1::2]]


DIFF_SLOPES, SWA_SLOPES = _alibi_slopes()


def _params(n_parallel):
    return pltpu.CompilerParams(dimension_semantics=("parallel",) * n_parallel,
                                vmem_limit_bytes=VMEM_LIMIT)


def _whole(space=pltpu.VMEM):
    return pl.BlockSpec(memory_space=space)


def _load_rows(x_ref):
    return jnp.concatenate([x_ref[c] for c in range(x_ref.shape[0])], axis=1)


def _store_rows(o_ref, val):
    for c in range(o_ref.shape[0]):
        o_ref[c] = val[:, c * LANES:(c + 1) * LANES]


def _row_spec(rows, index):
    return pl.BlockSpec((D_MODEL // LANES, rows, LANES), lambda i: (0, index(i), 0))


def _rmsnorm(x, gain):
    ms = jnp.mean(x * x, axis=-1, keepdims=True)
    return x * lax.rsqrt(ms + EPS) * gain


def _pair_rms_scale(y):
    lo = lax.broadcasted_iota(jnp.int32, y.shape, 1) < HEAD_DIM
    y2 = y * y
    s_lo = jnp.sum(jnp.where(lo, y2, 0.0), axis=-1, keepdims=True)
    s_hi = jnp.sum(jnp.where(lo, 0.0, y2), axis=-1, keepdims=True)
    ms = jnp.where(lo, s_lo, s_hi) * (1.0 / HEAD_DIM)
    return lax.rsqrt(ms + EPS)


def _nt_dot(a, b):
    return lax.dot_general(a, b, (((1,), (1,)), ((), ())), preferred_element_type=F32)


def _lo_lanes(rows):
    return lax.broadcasted_iota(jnp.int32, (rows, LANES), 1) < HEAD_DIM


def _own(lo, half):
    return lo if half == 0 else jnp.logical_not(lo)


def _pv_with_rowsum(p, v, own):
    return jnp.dot(p.astype(BF16), jnp.where(own, v, jnp.ones_like(v)), preferred_element_type=F32)


def _normalise(acc):
    return acc / pltpu.roll(acc, HEAD_DIM, 1)


def _in_proj_kernel(x_ref, gain_ref, w_ref, cos_ref, sin_ref, qn_ref, kn_ref, o_ref):
    h = _rmsnorm(_load_rows(x_ref), gain_ref[...]).astype(BF16)

    def proj(c0, width):
        return jnp.dot(h, w_ref[:, c0:c0 + width], preferred_element_type=F32)

    def plain(c0, width, scale=None):
        y = proj(c0, width)
        if scale is not None:
            y = y * scale
        o_ref[:, c0:c0 + width] = y.astype(BF16)

    def normed_rope(c0, gain_ref_, scale=None):
        y = proj(c0, LANES)
        yn = y * _pair_rms_scale(y) * gain_ref_[...]
        lane = lax.broadcasted_iota(jnp.int32, yn.shape, 1)
        partner = jnp.where((lane % 32) < 16, pltpu.roll(yn, LANES - 16, 1), pltpu.roll(yn, 16, 1))
        out = yn * cos_ref[...] + partner * sin_ref[...]
        if scale is not None:
            out = out * scale
        o_ref[:, c0:c0 + LANES] = out.astype(BF16)

    hd_scale = HEAD_DIM ** -0.5 * LOG2E
    plain(A_Q, 256, hd_scale)
    plain(A_K, 512)
    normed_rope(B_Q, qn_ref, hd_scale)
    normed_rope(B_Q + LANES, qn_ref, hd_scale)
    normed_rope(B_K, kn_ref)
    plain(B_V, 128)
    plain(C_Q, 256, DIFF_QK_DIM ** -0.5 * LOG2E)
    plain(C_K, 512)
    plain(D_Q, 256, hd_scale)
    plain(D_K, 256)


def _in_proj(x_cm, gain, w_qkv, cos_t, sin_t, qn, kn, seq):
    n_tok = x_cm.shape[1]
    t = TOKEN_TILE
    tiles_per_seq = seq // t
    return pl.pallas_call(
        _in_proj_kernel,
        grid=(n_tok // t,),
        in_specs=[
            _row_spec(t, lambda i: i),
            _whole(), _whole(),
            pl.BlockSpec((t, LANES), lambda i: (i % tiles_per_seq, 0)),
            pl.BlockSpec((t, LANES), lambda i: (i % tiles_per_seq, 0)),
            _whole(), _whole(),
        ],
        out_specs=pl.BlockSpec((t, QKV_COLS), lambda i: (i, 0)),
        out_shape=jax.ShapeDtypeStruct((n_tok, QKV_COLS), BF16),
        compiler_params=_params(1),
        name="in_proj",
    )(x_cm, gain, w_qkv, cos_t, sin_t, qn, kn)


def _na_kernel(q_ref, k_ref, v_ref, b_ref, o_ref, *, rows):
    t = pl.program_id(1)
    tq = NA_Q_ROWS * GRID_W
    tk = NA_K_ROWS * GRID_W
    r_start = jnp.clip(t * NA_Q_ROWS - NA_WIN_R // 2, 0, rows - NA_K_ROWS)
    k0 = pl.multiple_of(r_start * GRID_W, GRID_W)
    q = q_ref[0]
    kw = k_ref[0, pl.ds(k0, tk), :]
    vw = v_ref[0, pl.ds(k0, tk), :]
    lo_q, lo_k = _lo_lanes(tq), _lo_lanes(tk)
    for pair in range(2):
        sl = slice(pair * LANES, (pair + 1) * LANES)
        qp, kp, vp = q[:, sl], kw[:, sl], vw[:, sl]
        halves = []
        for half in range(2):
            qm = jnp.where(_own(lo_q, half), qp, jnp.zeros_like(qp))
            s = _nt_dot(qm, kp) + b_ref[0, 2 * pair + half]
            e = jnp.exp2(s - jnp.max(s, axis=-1, keepdims=True))
            halves.append(_normalise(_pv_with_rowsum(e, vp, _own(lo_k, half))))
        o_ref[0, :, sl] = jnp.where(lo_q, halves[0], halves[1]).astype(BF16)


def _na_bias_tables(rpb, rows):
    depth, heads = rpb.shape[0], rpb.shape[1]
    qc, kc = np.arange(GRID_W)[:, None], np.arange(GRID_W)[None, :]
    c0 = np.clip(qc - NA_WIN_C // 2, 0, GRID_W - NA_WIN_C)
    col_valid = (kc >= c0) & (kc < c0 + NA_WIN_C)
    dc = np.clip(kc - qc + NA_WIN_C - 1, 0, 2 * NA_WIN_C - 2)
    blocks = jnp.where(col_valid, rpb[:, :, :, dc] * LOG2E, NEG_INF).astype(F32)
    masked = jnp.full((depth, heads, GRID_W, GRID_W), NEG_INF, F32)
    cases = []
    for q_row0 in (0, NA_Q_ROWS, rows - NA_Q_ROWS):
        r_start = int(np.clip(q_row0 - NA_WIN_R // 2, 0, rows - NA_K_ROWS))
        strips = []
        for i in range(NA_Q_ROWS):
            qr = q_row0 + i
            r0 = int(np.clip(qr - NA_WIN_R // 2, 0, rows - NA_WIN_R))
            strip = []
            for j in range(NA_K_ROWS):
                kr = r_start + j
                strip.append(blocks[:, :, kr - qr + NA_WIN_R - 1] if r0 <= kr < r0 + NA_WIN_R else masked)
            strips.append(jnp.concatenate(strip, axis=-1))
        cases.append(jnp.concatenate(strips, axis=-2))
    return jnp.stack(cases, axis=1)


def _na_attention(qkv, bias, batch, seq):
    rows = seq // GRID_W
    n_t = rows // NA_Q_ROWS
    tq, tk = NA_Q_ROWS * GRID_W, NA_K_ROWS * GRID_W
    case = lambda t: jnp.where(t == 0, 0, jnp.where(t == n_t - 1, 2, 1))
    return pl.pallas_call(
        functools.partial(_na_kernel, rows=rows),
        grid=(batch, n_t),
        in_specs=[
            pl.BlockSpec((1, tq, 256), lambda b, t: (b, t, A_Q // 256)),
            pl.BlockSpec((1, seq, 256), lambda b, t: (b, 0, A_K // 256)),
            pl.BlockSpec((1, seq, 256), lambda b, t: (b, 0, A_V // 256)),
            pl.BlockSpec((1, 4, tq, tk), lambda b, t: (case(t), 0, 0, 0)),
        ],
        out_specs=pl.BlockSpec((1, tq, BRANCH_W), lambda b, t: (b, t, 0)),
        out_shape=jax.ShapeDtypeStruct((batch, seq, BRANCH_W), BF16),
        compiler_params=_params(2),
        name="na_attention",
    )(qkv, qkv, qkv, bias)


def _online_softmax_step(s, pv_fn, carry):
    m, acc = carry
    m_new = jnp.maximum(m, jnp.max(s, axis=-1, keepdims=True))
    alpha = jnp.exp2(m - m_new)
    return m_new, alpha * acc + pv_fn(jnp.exp2(s - m_new))


def _softmax_init(rows):
    return (jnp.full((rows, 1), NEG_INF, F32), jnp.zeros((rows, LANES), F32))


def _stack_group(q, sel):
    g0, g1 = q[:, :LANES], q[:, LANES:]
    zero = jnp.zeros_like(g0)
    return jnp.concatenate([jnp.where(sel, g0, zero), jnp.where(sel, g1, zero)], axis=0)


def _gqa_kernel(q_ref, k_ref, v_ref, o_ref, *, seq):
    tq, tk = GQA_TQ, GQA_TK
    q = q_ref[0]
    lo_q, lo_k = _lo_lanes(tq), _lo_lanes(tk)
    qq = [_stack_group(q, _own(lo_q, h)) for h in range(2)]

    def body(j, carry):
        k0 = pl.multiple_of(j * tk, tk)
        kj = k_ref[0, pl.ds(k0, tk), :]
        vj = v_ref[0, pl.ds(k0, tk), :]
        return tuple(
            _online_softmax_step(_nt_dot(qq[h], kj),
                                 functools.partial(_pv_with_rowsum, v=vj, own=_own(lo_k, h)), carry[h])
            for h in range(2))

    res = lax.fori_loop(0, seq // tk, body, (_softmax_init(2 * tq), _softmax_init(2 * tq)), unroll=2)
    n = [_normalise(acc) for _, acc in res]
    o_ref[0, :, :LANES] = jnp.where(lo_q, n[0][:tq], n[1][:tq]).astype(BF16)
    o_ref[0, :, LANES:] = jnp.where(lo_q, n[0][tq:], n[1][tq:]).astype(BF16)


def _gqa_attention(qkv, batch, seq):
    tq = GQA_TQ
    return pl.pallas_call(
        functools.partial(_gqa_kernel, seq=seq),
        grid=(batch, seq // tq),
        in_specs=[
            pl.BlockSpec((1, tq, 256), lambda b, t: (b, t, B_Q // 256)),
            pl.BlockSpec((1, seq, LANES), lambda b, t: (b, 0, B_K // LANES)),
            pl.BlockSpec((1, seq, LANES), lambda b, t: (b, 0, B_V // LANES)),
        ],
        out_specs=pl.BlockSpec((1, tq, BRANCH_W), lambda b, t: (b, t, 0)),
        out_shape=jax.ShapeDtypeStruct((batch, seq, BRANCH_W), BF16),
        compiler_params=_params(2),
        name="gqa_attention",
    )(qkv, qkv, qkv)


_AUG_KEY_LO, _AUG_KEY_HI, _AUG_ROW, _AUG_DELTA = 0, 3, 6, 9
_AUG_MASK = 12


def _bf16_pieces(x):
    x = np.asarray(x, np.float32)
    pieces = []
    for _ in range(3):
        p = x.astype(BF16).astype(np.float32)
        pieces.append(p)
        x = x - p
    return pieces


def _diff_bias_factors():
    tq, tk = DIFF_TQ, DIFF_TK
    q_side = np.zeros((4, tq, LANES), np.float32)
    for h, slope in enumerate(DIFF_SLOPES):
        c = np.float32(slope * LOG2E)
        c3 = _bf16_pieces(c)
        r3 = _bf16_pieces(-c * np.arange(tq, dtype=np.float32))
        for n in range(3):
            for g in (_AUG_KEY_LO, _AUG_KEY_HI, _AUG_DELTA):
                q_side[h, :, g + n] = c3[n]
            q_side[h, :, _AUG_ROW + n] = r3[n]
        q_side[h, :, _AUG_MASK] = 1.0
    k_side = np.zeros((tk, LANES), np.float32)
    j = np.arange(tk)
    for n in range(3):
        k_side[:, _AUG_KEY_LO + n] = j % 256
        k_side[:, _AUG_KEY_HI + n] = 256 * (j // 256)
        k_side[:, _AUG_ROW + n] = 1.0
    return jnp.asarray(q_side, BF16), jnp.asarray(k_side, F32)


def _diff_kernel(q_ref, k_ref, v_ref, qf_ref, kf_ref, lq1_ref, lk1_ref, lq2_ref, lk2_ref, subln_ref, o_ref, *,
                 seq, lambda_init):
    tq, tk = DIFF_TQ, DIFF_TK
    q_start = pl.multiple_of(pl.program_id(1) * tq, tq)
    q = q_ref[0]
    lam = (jnp.exp(jnp.sum(lq1_ref[...] * lk1_ref[...], axis=-1, keepdims=True))
           - jnp.exp(jnp.sum(lq2_ref[...] * lk2_ref[...], axis=-1, keepdims=True)) + lambda_init)
    lane = lax.broadcasted_iota(jnp.int32, (tq, LANES), 1)
    seg = lane // DIFF_QK_DIM
    lo_q, lo_k = lane < HEAD_DIM, _lo_lanes(tk)
    lane_k = lax.broadcasted_iota(jnp.int32, (tk, LANES), 1)
    key_k = lax.broadcasted_iota(jnp.int32, (tk, LANES), 0)
    delta_lanes = (lane_k >= _AUG_DELTA) & (lane_k < _AUG_DELTA + 3)
    mask_lane = lane_k == _AUG_MASK
    own_dist = jnp.abs(lax.broadcasted_iota(jnp.int32, (tq, tq), 0)
                       - lax.broadcasted_iota(jnp.int32, (tq, tq), 1)).astype(F32)

    def pv(p, v, lo):
        return jnp.concatenate([_pv_with_rowsum(p[:2 * tq], v, lo),
                                _pv_with_rowsum(p[2 * tq:], v, jnp.logical_not(lo))], axis=0)

    qq_aug, carry = [], []
    for pair in range(2):
        sl = slice(pair * LANES, (pair + 1) * LANES)
        qp = q[:, sl]
        zero = jnp.zeros_like(qp)
        rows4 = jnp.concatenate([jnp.where(seg == i, qp, zero) for i in range(4)], axis=0)
        factors = jnp.concatenate([qf_ref[2 * pair]] * 2 + [qf_ref[2 * pair + 1]] * 2, axis=0)
        qq_aug.append(jnp.concatenate([rows4, factors], axis=1))
        bias = jnp.concatenate([(DIFF_SLOPES[2 * pair] * LOG2E) * own_dist] * 2
                               + [(DIFF_SLOPES[2 * pair + 1] * LOG2E) * own_dist] * 2, axis=0)
        s = _nt_dot(rows4, k_ref[0, pl.ds(q_start, tq), sl]) - bias
        m = jnp.max(s, axis=-1, keepdims=True)
        carry.append((m, pv(jnp.exp2(s - m), v_ref[0, pl.ds(q_start, tq), sl], lo_q)))

    def body(j, carry):
        k0 = pl.multiple_of(j * tk, tk)
        rel = key_k + (k0 - q_start)
        sign = jnp.where(rel < 0, 1.0, -1.0)
        kf = sign * (kf_ref[...] - jnp.where(delta_lanes, (q_start - k0).astype(F32), 0.0))
        own_key = jnp.where(rel >= 0, jnp.where(rel < tq, NEG_INF, 0.0), 0.0)
        k_factors = jnp.where(mask_lane, own_key, kf).astype(BF16)
        new = []
        for pair in range(2):
            sl = slice(pair * LANES, (pair + 1) * LANES)
            s = _nt_dot(qq_aug[pair], jnp.concatenate([k_ref[0, pl.ds(k0, tk), sl], k_factors], axis=1))
            new.append(_online_softmax_step(
                s, functools.partial(pv, v=v_ref[0, pl.ds(k0, tk), sl], lo=lo_k), carry[pair]))
        return tuple(new)

    carry = lax.fori_loop(0, seq // tk, body, tuple(carry), unroll=2)
    for pair in range(2):
        p = _normalise(carry[pair][1])
        head_a = p[0:tq] - lam * p[tq:2 * tq]
        head_b = p[2 * tq:3 * tq] - lam * p[3 * tq:4 * tq]
        o = jnp.where(lo_q, head_a, head_b)
        o = o * _pair_rms_scale(o) * subln_ref[...] * (1.0 - lambda_init)
        o_ref[0, :, pair * LANES:(pair + 1) * LANES] = o.astype(BF16)


def _diff_attention(qkv, q_factors, k_factors, lq1, lk1, lq2, lk2, subln, batch, seq, lambda_init):
    tq = DIFF_TQ
    return pl.pallas_call(
        functools.partial(_diff_kernel, seq=seq, lambda_init=lambda_init),
        grid=(batch, seq // tq),
        in_specs=[
            pl.BlockSpec((1, tq, 256), lambda b, t: (b, t, C_Q // 256)),
            pl.BlockSpec((1, seq, 256), lambda b, t: (b, 0, C_K // 256)),
            pl.BlockSpec((1, seq, 256), lambda b, t: (b, 0, C_V // 256)),
        ] + [_whole()] * 7,
        out_specs=pl.BlockSpec((1, tq, BRANCH_W), lambda b, t: (b, t, 0)),
        out_shape=jax.ShapeDtypeStruct((batch, seq, BRANCH_W), BF16),
        compiler_params=_params(2),
        name="diff_attention",
    )(qkv, qkv, qkv, q_factors, k_factors, lq1, lk1, lq2, lk2, subln)


def _swa_kernel(q_ref, k_ref, v_ref, sink_ref, o_ref, *, seq):
    tq = SWA_TQ
    span = tq + 2 * SWA_WINDOW
    q_start = pl.program_id(1) * tq
    k_start = pl.multiple_of(jnp.clip(q_start - SWA_WINDOW, 0, seq - span), SWA_WINDOW)
    q = q_ref[0]
    kw = k_ref[0, pl.ds(k_start, span), :]
    vw = v_ref[0, pl.ds(k_start, span), :]
    q_pos = q_start + lax.broadcasted_iota(jnp.int32, (tq, span), 0)
    k_pos = k_start + lax.broadcasted_iota(jnp.int32, (tq, span), 1)
    rel = jnp.abs(k_pos - q_pos)
    valid = rel <= SWA_WINDOW
    rel_f = rel.astype(F32)
    lo_q, lo_k = _lo_lanes(tq), _lo_lanes(span)
    outs = [[None, None], [None, None]]
    for kv_head in range(2):
        s_all = _nt_dot(_stack_group(q, _own(lo_q, kv_head)), kw)
        for g in range(2):
            head = 2 * kv_head + g
            sink = sink_ref[head] * LOG2E
            s = s_all[g * tq:(g + 1) * tq] - (SWA_SLOPES[head] * LOG2E) * rel_f
            s = jnp.where(valid, s, NEG_INF)
            m = jnp.maximum(jnp.max(s, axis=-1, keepdims=True), sink)
            r = _pv_with_rowsum(jnp.exp2(s - m), vw, _own(lo_k, kv_head))
            outs[g][kv_head] = r / (pltpu.roll(r, HEAD_DIM, 1) + jnp.exp2(sink - m))
    o_ref[0, :, :LANES] = jnp.where(lo_q, outs[0][0], outs[0][1]).astype(BF16)
    o_ref[0, :, LANES:] = jnp.where(lo_q, outs[1][0], outs[1][1]).astype(BF16)


def _swa_attention(qkv, sink, batch, seq):
    tq = SWA_TQ
    return pl.pallas_call(
        functools.partial(_swa_kernel, seq=seq),
        grid=(batch, seq // tq),
        in_specs=[
            pl.BlockSpec((1, tq, 256), lambda b, t: (b, t, D_Q // 256)),
            pl.BlockSpec((1, seq, LANES), lambda b, t: (b, 0, D_K // LANES)),
            pl.BlockSpec((1, seq, LANES), lambda b, t: (b, 0, D_V // LANES)),
            _whole(pltpu.SMEM),
        ],
        out_specs=pl.BlockSpec((1, tq, BRANCH_W), lambda b, t: (b, t, 0)),
        out_shape=jax.ShapeDtypeStruct((batch, seq, BRANCH_W), BF16),
        compiler_params=_params(2),
        name="swa_attention",
    )(qkv, qkv, qkv, sink)


def _merge_kernel(x_ref, oa_ref, ob_ref, oc_ref, od_ref, gpre_ref, wg_ref, wb_ref, wo_ref, gpost_ref, o_ref):
    x = _load_rows(x_ref)
    h = _rmsnorm(x, gpre_ref[...]).astype(BF16)
    merged = None
    for i, br_ref in enumerate((oa_ref, ob_ref, oc_ref, od_ref)):
        gate = jax.nn.sigmoid(jnp.dot(h, wg_ref[:, i * D_MODEL:(i + 1) * D_MODEL], preferred_element_type=F32))
        term = gate * jnp.dot(br_ref[...], wb_ref[i], preferred_element_type=F32)
        merged = term if merged is None else merged + term
    mix = jnp.dot(merged.astype(BF16), wo_ref[...], preferred_element_type=F32)
    _store_rows(o_ref, x + _rmsnorm(mix, gpost_ref[...]))


def _merge(x_cm, branches, gpre, w_gate, w_branch, w_out, gpost):
    n_tok = x_cm.shape[1]
    t = TOKEN_TILE
    tok = lambda w: pl.BlockSpec((t, w), lambda i: (i, 0))
    return pl.pallas_call(
        _merge_kernel,
        grid=(n_tok // t,),
        in_specs=[_row_spec(t, lambda i: i)] + [tok(BRANCH_W)] * 4 + [_whole()] * 5,
        out_specs=_row_spec(t, lambda i: i),
        out_shape=jax.ShapeDtypeStruct(x_cm.shape, F32),
        compiler_params=_params(1),
        name="merge",
    )(x_cm, *branches, gpre, w_gate, w_branch, w_out, gpost)


def _gelu_tanh(x):
    k = -2.0 * LOG2E * math.sqrt(2.0 / math.pi)
    return x / (1.0 + jnp.exp2(x * (k + (k * 0.044715) * (x * x))))


def _ffn_kernel(xp_ref, x_ref, xn_ref, gpre_ref, wup_ref, cw_ref, cb_ref, wdn_ref, gpost_ref, o_ref,
                h_scr, a_scr, *, tiles_per_seq):
    t = TOKEN_TILE
    n_b = t // 8
    pos = pl.program_id(0) % tiles_per_seq
    gpre = gpre_ref[...]
    n_c = D_MODEL // LANES
    x = jnp.concatenate(
        [jnp.concatenate([x_ref[c, pl.ds(b, 8, stride=n_b), :] for b in range(n_b)], axis=0) for c in range(n_c)],
        axis=1)
    h_scr[0:t, :] = _rmsnorm(x, gpre).astype(BF16)
    keep_prev = (pos != 0).astype(F32)
    keep_next = (pos != tiles_per_seq - 1).astype(F32)
    hp = pltpu.roll(_rmsnorm(_load_rows(xp_ref), gpre) * keep_prev, 1, 0)
    hn = pltpu.roll(_rmsnorm(_load_rows(xn_ref), gpre) * keep_next, 1, 0)
    hrow = lax.broadcasted_iota(jnp.int32, (HALO, D_MODEL), 0)
    h_scr[t:, :] = jnp.where(hrow == 0, hp, jnp.where(hrow == 1, hn, 0.0)).astype(BF16)
    h = h_scr[...]
    sub = lax.broadcasted_iota(jnp.int32, (8, FFN_CHUNK), 0)

    def conv(c0):
        u = jnp.dot(h, wup_ref[:, c0:c0 + FFN_CHUNK], preferred_element_type=F32)
        halo = u[t:t + 8]
        first_prev = jnp.where(sub == 0, halo[0:1], pltpu.roll(u[t - 8:t], 1, 0))
        last_next = jnp.where(sub == 7, halo[1:2], pltpu.roll(u[0:8], 7, 0))
        prev = jnp.concatenate([first_prev, u[0:t - 8]], axis=0)
        nxt = jnp.concatenate([u[8:t], last_next], axis=0)
        w = cw_ref[:, c0:c0 + FFN_CHUNK]
        return cb_ref[:, c0:c0 + FFN_CHUNK] + prev * w[0:1] + u[0:t] * w[1:2] + nxt * w[2:3]

    for ch in range(D_FF // FFN_CHUNK):
        gate = conv(ch * FFN_CHUNK)
        val = conv(D_FF + ch * FFN_CHUNK)
        a_scr[:, ch * FFN_CHUNK:(ch + 1) * FFN_CHUNK] = (_gelu_tanh(gate) * val).astype(BF16)
    y = jnp.dot(a_scr[...], wdn_ref[...], preferred_element_type=F32)
    out = x + _rmsnorm(y, gpost_ref[...])
    for c in range(n_c):
        for b in range(n_b):
            o_ref[c, pl.ds(b, 8, stride=n_b), :] = out[8 * b:8 * b + 8, c * LANES:(c + 1) * LANES]


def _ffn(x_cm, gpre, w_up, conv_w, conv_b, w_down, gpost, seq):
    n_tok = x_cm.shape[1]
    t = TOKEN_TILE
    per_tile = t // HALO
    n_halo_blocks = n_tok // HALO
    return pl.pallas_call(
        functools.partial(_ffn_kernel, tiles_per_seq=seq // t),
        grid=(n_tok // t,),
        in_specs=[
            _row_spec(HALO, lambda i: jnp.maximum(i * per_tile - 1, 0)),
            _row_spec(t, lambda i: i),
            _row_spec(HALO, lambda i: jnp.minimum((i + 1) * per_tile, n_halo_blocks - 1)),
        ] + [_whole()] * 6,
        out_specs=_row_spec(t, lambda i: i),
        out_shape=jax.ShapeDtypeStruct(x_cm.shape, F32),
        scratch_shapes=[
            pltpu.VMEM((t + HALO, D_MODEL), BF16),
            pltpu.VMEM((t, D_FF), BF16),
        ],
        compiler_params=_params(1),
        name="conv_ffn",
    )(x_cm, x_cm, x_cm, gpre, w_up, conv_w, conv_b, w_down, gpost)


def _rope_tables(seq):
    t = np.arange(seq)
    axis_dim = HEAD_DIM // 2
    inv = jnp.asarray(ROPE_THETA, F32) ** (-jnp.arange(0, axis_dim, 2, dtype=F32) / axis_dim)
    ang_r = jnp.asarray(t // GRID_W, F32)[:, None] * inv
    ang_c = jnp.asarray(t % GRID_W, F32)[:, None] * inv
    cos = jnp.concatenate([jnp.cos(ang_r)] * 2 + [jnp.cos(ang_c)] * 2, axis=-1)
    sin = jnp.concatenate([-jnp.sin(ang_r), jnp.sin(ang_r), -jnp.sin(ang_c), jnp.sin(ang_c)], axis=-1)
    return jnp.tile(cos, (1, 2)), jnp.tile(sin, (1, 2))


def _group_major(w, axis):
    shape = w.shape
    w = w.reshape(shape[:axis] + (2, 2, HEAD_DIM) + shape[axis + 1:])
    return jnp.swapaxes(w, axis, axis + 1).reshape(shape)


def kernel(x_prompt, x_sample, norm_mix_pre, norm_mix_post, norm_ffn_pre, norm_ffn_post, w_in, na_rpb, gqa_q_norm, gqa_k_norm, diff_lambda_q1, diff_lambda_k1, diff_lambda_q2, diff_lambda_k2, diff_subln, swa_sink, w_branch, w_out, ffn_w_up, ffn_conv_w, ffn_conv_b, ffn_w_down):
    depth = w_in.shape[0]
    w_qkv = jnp.concatenate([
        w_in[:, :, :B_Q], _group_major(w_in[:, :, B_Q:B_K], 2), w_in[:, :, B_K:D_Q],
        _group_major(w_in[:, :, D_Q:D_K], 2), w_in[:, :, D_K:QKV_COLS]], axis=-1).astype(BF16)
    w_gate = w_in[:, :, QKV_COLS:].astype(BF16)
    w_br = w_branch.reshape(depth, N_BRANCH, BRANCH_W, D_MODEL)
    w_br = jnp.stack([w_br[:, 0], _group_major(w_br[:, 1], 1), w_br[:, 2], _group_major(w_br[:, 3], 1)],
                     axis=1).astype(BF16)
    diff_qf, diff_kf = _diff_bias_factors()
    w_o = w_out.astype(BF16)
    w_up = ffn_w_up.astype(BF16)
    w_dn = ffn_w_down.astype(BF16)
    row2 = lambda a: a.astype(F32).reshape(depth, 1, -1)
    tile2 = lambda a: jnp.tile(a.astype(F32), (1, 2)).reshape(depth, 1, -1)
    g_mix_pre, g_mix_post = row2(norm_mix_pre), row2(norm_mix_post)
    g_ffn_pre, g_ffn_post = row2(norm_ffn_pre), row2(norm_ffn_post)
    qn, kn, subln = tile2(gqa_q_norm), tile2(gqa_k_norm), tile2(diff_subln)
    lq1, lk1, lq2, lk2 = (row2(a) for a in (diff_lambda_q1, diff_lambda_k1, diff_lambda_q2, diff_lambda_k2))
    conv_b = row2(ffn_conv_b)
    conv_w = ffn_conv_w.astype(F32)
    sink = swa_sink.astype(F32)

    def run_trunk(x):
        batch, seq, _ = x.shape
        cos_t, sin_t = _rope_tables(seq)
        na_bias = _na_bias_tables(na_rpb.astype(F32), seq // GRID_W)
        x_cm = x.reshape(batch * seq, D_MODEL // LANES, LANES).transpose(1, 0, 2)
        for l in range(depth):
            lambda_init = 0.8 - 0.6 * math.exp(-0.3 * l)
            qkv = _in_proj(x_cm, g_mix_pre[l], w_qkv[l], cos_t, sin_t, qn[l], kn[l], seq)
            qkv = qkv.reshape(batch, seq, QKV_COLS)
            o_a = _na_attention(qkv, na_bias[l], batch, seq)
            o_b = _gqa_attention(qkv, batch, seq)
            o_c = _diff_attention(qkv, diff_qf, diff_kf, lq1[l], lk1[l], lq2[l], lk2[l], subln[l], batch, seq,
                                  lambda_init)
            o_d = _swa_attention(qkv, sink[l], batch, seq)
            branches = [o.reshape(batch * seq, BRANCH_W) for o in (o_a, o_b, o_c, o_d)]
            x_cm = _merge(x_cm, branches, g_mix_pre[l], w_gate[l], w_br[l], w_o[l], g_mix_post[l])
            x_cm = _ffn(x_cm, g_ffn_pre[l], w_up[l], conv_w[l], conv_b[l], w_dn[l], g_ffn_post[l], seq)
        return x_cm.transpose(1, 0, 2).reshape(batch, seq, D_MODEL)

    return (run_trunk(x_prompt), run_trunk(x_sample))
```

```python
import functools
import math

import numpy as np
import jax
import jax.numpy as jnp
from jax import lax
from jax.experimental import pallas as pl
from jax.experimental.pallas import tpu as pltpu

F32 = jnp.float32
BF16 = jnp.bfloat16

D_MODEL = 1024
GRID_W = 64
HEAD_DIM = 64
EPS = 1e-6
NEG_INF = -1e30
LOG2E = math.log2(math.e)
NA_WIN_R = 8
NA_WIN_C = 16
ROPE_THETA = 10000.0
DIFF_QK_DIM = 32
SWA_WINDOW = 128
N_ALIBI_HEADS = 8
N_BRANCH = 4
BRANCH_W = 256
D_FF = 2816
QKV_COLS = 2560

A_Q, A_K, A_V = 0, 256, 512
B_Q, B_K, B_V = 768, 1024, 1152
C_Q, C_K, C_V = 1280, 1536, 1792
D_Q, D_K, D_V = 2048, 2304, 2432

LANES = 128
VMEM_LIMIT = 56 * 1024 * 1024

TOKEN_TILE = 512
HALO = 16
FFN_CHUNK = 256
NA_Q_ROWS = 4
NA_K_ROWS = 12
GQA_TQ, GQA_TK = 512, 2048
DIFF_TQ, DIFF_TK = 256, 2048
SWA_TQ = 256


def _alibi_slopes():
    s = 2.0 ** (-8.0 * np.arange(1, N_ALIBI_HEADS + 1) / N_ALIBI_HEADS)
    return [float(v) for v in s[0::2]], [float(v) for v in s[1::2]]


DIFF_SLOPES, SWA_SLOPES = _alibi_slopes()


def _params(n_parallel):
    return pltpu.CompilerParams(dimension_semantics=("parallel",) * n_parallel,
                                vmem_limit_bytes=VMEM_LIMIT)


def _whole(space=pltpu.VMEM):
    return pl.BlockSpec(memory_space=space)


def _load_rows(x_ref):
    if len(x_ref.shape) == 2:
        return x_ref[...]
    return jnp.concatenate([x_ref[c] for c in range(x_ref.shape[0])], axis=1)


def _store_rows(o_ref, val):
    for c in range(o_ref.shape[0]):
        o_ref[c] = val[:, c * LANES:(c + 1) * LANES]


def _row_spec(rows, index, chunked=True):
    if not chunked:
        return pl.BlockSpec((rows, D_MODEL), lambda i: (index(i), 0))
    return pl.BlockSpec((D_MODEL // LANES, rows, LANES), lambda i: (0, index(i), 0))


def _rmsnorm(x, gain):
    ms = jnp.mean(x * x, axis=-1, keepdims=True)
    return x * lax.rsqrt(ms + EPS) * gain


def _pair_rms_scale(y):
    lo = lax.broadcasted_iota(jnp.int32, y.shape, 1) < HEAD_DIM
    y2 = y * y
    s_lo = jnp.sum(jnp.where(lo, y2, 0.0), axis=-1, keepdims=True)
    s_hi = jnp.sum(jnp.where(lo, 0.0, y2), axis=-1, keepdims=True)
    ms = jnp.where(lo, s_lo, s_hi) * (1.0 / HEAD_DIM)
    return lax.rsqrt(ms + EPS)


def _nt_dot(a, b):
    return lax.dot_general(a, b, (((1,), (1,)), ((), ())), preferred_element_type=F32)


def _lo_lanes(rows):
    return lax.broadcasted_iota(jnp.int32, (rows, LANES), 1) < HEAD_DIM


def _own(lo, half):
    return lo if half == 0 else jnp.logical_not(lo)


def _pv_with_rowsum(p, v, own):
    return jnp.dot(p.astype(BF16), jnp.where(own, v, jnp.ones_like(v)), preferred_element_type=F32)


def _normalise(acc):
    return acc / pltpu.roll(acc, HEAD_DIM, 1)


def _in_proj_kernel(x_ref, gain_ref, w_ref, cos_ref, sin_ref, qn_ref, kn_ref, o_ref):
    h = _rmsnorm(_load_rows(x_ref), gain_ref[...]).astype(BF16)

    def proj(c0, width):
        return jnp.dot(h, w_ref[:, c0:c0 + width], preferred_element_type=F32)

    def plain(c0, width, scale=None):
        y = proj(c0, width)
        if scale is not None:
            y = y * scale
        o_ref[:, c0:c0 + width] = y.astype(BF16)

    def normed_rope(c0, gain_ref_, scale=None):
        y = proj(c0, LANES)
        yn = y * _pair_rms_scale(y) * gain_ref_[...]
        lane = lax.broadcasted_iota(jnp.int32, yn.shape, 1)
        partner = jnp.where((lane % 32) < 16, pltpu.roll(yn, LANES - 16, 1), pltpu.roll(yn, 16, 1))
        out = yn * cos_ref[...] + partner * sin_ref[...]
        if scale is not None:
            out = out * scale
        o_ref[:, c0:c0 + LANES] = out.astype(BF16)

    hd_scale = HEAD_DIM ** -0.5 * LOG2E
    plain(A_Q, 256, hd_scale)
    plain(A_K, 512)
    normed_rope(B_Q, qn_ref, hd_scale)
    normed_rope(B_Q + LANES, qn_ref, hd_scale)
    normed_rope(B_K, kn_ref)
    plain(B_V, 128)
    plain(C_Q, 256, DIFF_QK_DIM ** -0.5 * LOG2E)
    plain(C_K, 512)
    plain(D_Q, 256, hd_scale)
    plain(D_K, 256)


def _in_proj(x_cm, gain, w_qkv, cos_t, sin_t, qn, kn, seq):
    chunked = x_cm.ndim == 3
    n_tok = x_cm.shape[-2]
    t = TOKEN_TILE
    tiles_per_seq = seq // t
    return pl.pallas_call(
        _in_proj_kernel,
        grid=(n_tok // t,),
        in_specs=[
            _row_spec(t, lambda i: i, chunked),
            _whole(), _whole(),
            pl.BlockSpec((t, LANES), lambda i: (i % tiles_per_seq, 0)),
            pl.BlockSpec((t, LANES), lambda i: (i % tiles_per_seq, 0)),
            _whole(), _whole(),
        ],
        out_specs=pl.BlockSpec((t, QKV_COLS), lambda i: (i, 0)),
        out_shape=jax.ShapeDtypeStruct((n_tok, QKV_COLS), BF16),
        compiler_params=_params(1),
        name="in_proj",
    )(x_cm, gain, w_qkv, cos_t, sin_t, qn, kn)


def _na_kernel(q_ref, k_ref, v_ref, b_ref, o_ref, *, rows):
    t = pl.program_id(1)
    tq = NA_Q_ROWS * GRID_W
    tk = NA_K_ROWS * GRID_W
    r_start = jnp.clip(t * NA_Q_ROWS - NA_WIN_R // 2, 0, rows - NA_K_ROWS)
    k0 = pl.multiple_of(r_start * GRID_W, GRID_W)
    q = q_ref[0]
    kw = k_ref[0, pl.ds(k0, tk), :]
    vw = v_ref[0, pl.ds(k0, tk), :]
    lo_q, lo_k = _lo_lanes(tq), _lo_lanes(tk)
    for pair in range(2):
        sl = slice(pair * LANES, (pair + 1) * LANES)
        qp, kp, vp = q[:, sl], kw[:, sl], vw[:, sl]
        halves = []
        for half in range(2):
            qm = jnp.where(_own(lo_q, half), qp, jnp.zeros_like(qp))
            s = _nt_dot(qm, kp) + b_ref[0, 2 * pair + half]
            e = jnp.exp2(s - jnp.max(s, axis=-1, keepdims=True))
            halves.append(_normalise(_pv_with_rowsum(e, vp, _own(lo_k, half))))
        o_ref[0, :, sl] = jnp.where(lo_q, halves[0], halves[1]).astype(BF16)


def _na_bias_tables(rpb, rows):
    depth, heads = rpb.shape[0], rpb.shape[1]
    qc, kc = np.arange(GRID_W)[:, None], np.arange(GRID_W)[None, :]
    c0 = np.clip(qc - NA_WIN_C // 2, 0, GRID_W - NA_WIN_C)
    col_valid = (kc >= c0) & (kc < c0 + NA_WIN_C)
    dc = np.clip(kc - qc + NA_WIN_C - 1, 0, 2 * NA_WIN_C - 2)
    blocks = jnp.where(col_valid, rpb[:, :, :, dc] * LOG2E, NEG_INF).astype(F32)
    masked = jnp.full((depth, heads, GRID_W, GRID_W), NEG_INF, F32)
    cases = []
    for q_row0 in (0, NA_Q_ROWS, rows - NA_Q_ROWS):
        r_start = int(np.clip(q_row0 - NA_WIN_R // 2, 0, rows - NA_K_ROWS))
        strips = []
        for i in range(NA_Q_ROWS):
            qr = q_row0 + i
            r0 = int(np.clip(qr - NA_WIN_R // 2, 0, rows - NA_WIN_R))
            strip = []
            for j in range(NA_K_ROWS):
                kr = r_start + j
                strip.append(blocks[:, :, kr - qr + NA_WIN_R - 1] if r0 <= kr < r0 + NA_WIN_R else masked)
            strips.append(jnp.concatenate(strip, axis=-1))
        cases.append(jnp.concatenate(strips, axis=-2))
    return jnp.stack(cases, axis=1)


def _na_attention(qkv, bias, batch, seq):
    rows = seq // GRID_W
    n_t = rows // NA_Q_ROWS
    tq, tk = NA_Q_ROWS * GRID_W, NA_K_ROWS * GRID_W
    case = lambda t: jnp.where(t == 0, 0, jnp.where(t == n_t - 1, 2, 1))
    return pl.pallas_call(
        functools.partial(_na_kernel, rows=rows),
        grid=(batch, n_t),
        in_specs=[
            pl.BlockSpec((1, tq, 256), lambda b, t: (b, t, A_Q // 256)),
            pl.BlockSpec((1, seq, 256), lambda b, t: (b, 0, A_K // 256)),
            pl.BlockSpec((1, seq, 256), lambda b, t: (b, 0, A_V // 256)),
            pl.BlockSpec((1, 4, tq, tk), lambda b, t: (case(t), 0, 0, 0)),
        ],
        out_specs=pl.BlockSpec((1, tq, BRANCH_W), lambda b, t: (b, t, 0)),
        out_shape=jax.ShapeDtypeStruct((batch, seq, BRANCH_W), BF16),
        compiler_params=_params(2),
        name="na_attention",
    )(qkv, qkv, qkv, bias)


def _online_softmax_step(s, pv_fn, carry):
    m, acc = carry
    m_new = jnp.maximum(m, jnp.max(s, axis=-1, keepdims=True))
    alpha = jnp.exp2(m - m_new)
    return m_new, alpha * acc + pv_fn(jnp.exp2(s - m_new))


def _softmax_init(rows):
    return (jnp.full((rows, 1), NEG_INF, F32), jnp.zeros((rows, LANES), F32))


def _stack_group(q, sel):
    g0, g1 = q[:, :LANES], q[:, LANES:]
    zero = jnp.zeros_like(g0)
    return jnp.concatenate([jnp.where(sel, g0, zero), jnp.where(sel, g1, zero)], axis=0)


def _gqa_kernel(q_ref, k_ref, v_ref, o_ref, *, seq):
    tq, tk = GQA_TQ, GQA_TK
    q = q_ref[0]
    lo_q, lo_k = _lo_lanes(tq), _lo_lanes(tk)
    qq = [_stack_group(q, _own(lo_q, h)) for h in range(2)]

    def body(j, carry):
        k0 = pl.multiple_of(j * tk, tk)
        kj = k_ref[0, pl.ds(k0, tk), :]
        vj = v_ref[0, pl.ds(k0, tk), :]
        return tuple(
            _online_softmax_step(_nt_dot(qq[h], kj),
                                 functools.partial(_pv_with_rowsum, v=vj, own=_own(lo_k, h)), carry[h])
            for h in range(2))

    res = lax.fori_loop(0, seq // tk, body, (_softmax_init(2 * tq), _softmax_init(2 * tq)), unroll=2)
    n = [_normalise(acc) for _, acc in res]
    o_ref[0, :, :LANES] = jnp.where(lo_q, n[0][:tq], n[1][:tq]).astype(BF16)
    o_ref[0, :, LANES:] = jnp.where(lo_q, n[0][tq:], n[1][tq:]).astype(BF16)


def _gqa_attention(qkv, batch, seq):
    tq = GQA_TQ
    return pl.pallas_call(
        functools.partial(_gqa_kernel, seq=seq),
        grid=(batch, seq // tq),
        in_specs=[
            pl.BlockSpec((1, tq, 256), lambda b, t: (b, t, B_Q // 256)),
            pl.BlockSpec((1, seq, LANES), lambda b, t: (b, 0, B_K // LANES)),
            pl.BlockSpec((1, seq, LANES), lambda b, t: (b, 0, B_V // LANES)),
        ],
        out_specs=pl.BlockSpec((1, tq, BRANCH_W), lambda b, t: (b, t, 0)),
        out_shape=jax.ShapeDtypeStruct((batch, seq, BRANCH_W), BF16),
        compiler_params=_params(2),
        name="gqa_attention",
    )(qkv, qkv, qkv)


_AUG_KEY_LO, _AUG_KEY_HI, _AUG_ROW, _AUG_DELTA = 0, 3, 6, 9
_AUG_MASK = 12


def _bf16_pieces(x):
    x = np.asarray(x, np.float32)
    pieces = []
    for _ in range(3):
        p = x.astype(BF16).astype(np.float32)
        pieces.append(p)
        x = x - p
    return pieces


def _diff_bias_factors():
    tq, tk = DIFF_TQ, DIFF_TK
    q_side = np.zeros((4, tq, LANES), np.float32)
    for h, slope in enumerate(DIFF_SLOPES):
        c = np.float32(slope * LOG2E)
        c3 = _bf16_pieces(c)
        r3 = _bf16_pieces(-c * np.arange(tq, dtype=np.float32))
        for n in range(3):
            for g in (_AUG_KEY_LO, _AUG_KEY_HI, _AUG_DELTA):
                q_side[h, :, g + n] = c3[n]
            q_side[h, :, _AUG_ROW + n] = r3[n]
        q_side[h, :, _AUG_MASK] = 1.0
    k_side = np.zeros((tk, LANES), np.float32)
    j = np.arange(tk)
    for n in range(3):
        k_side[:, _AUG_KEY_LO + n] = j % 256
        k_side[:, _AUG_KEY_HI + n] = 256 * (j // 256)
        k_side[:, _AUG_ROW + n] = 1.0
    return jnp.asarray(q_side, BF16), jnp.asarray(k_side, F32)


def _diff_kernel(q_ref, k_ref, v_ref, qf_ref, kf_ref, lq1_ref, lk1_ref, lq2_ref, lk2_ref, subln_ref, o_ref, *,
                 seq, lambda_init):
    tq, tk = DIFF_TQ, DIFF_TK
    q_start = pl.multiple_of(pl.program_id(1) * tq, tq)
    q = q_ref[0]
    lam = (jnp.exp(jnp.sum(lq1_ref[...] * lk1_ref[...], axis=-1, keepdims=True))
           - jnp.exp(jnp.sum(lq2_ref[...] * lk2_ref[...], axis=-1, keepdims=True)) + lambda_init)
    lane = lax.broadcasted_iota(jnp.int32, (tq, LANES), 1)
    seg = lane // DIFF_QK_DIM
    lo_q, lo_k = lane < HEAD_DIM, _lo_lanes(tk)
    lane_k = lax.broadcasted_iota(jnp.int32, (tk, LANES), 1)
    key_k = lax.broadcasted_iota(jnp.int32, (tk, LANES), 0)
    delta_lanes = (lane_k >= _AUG_DELTA) & (lane_k < _AUG_DELTA + 3)
    mask_lane = lane_k == _AUG_MASK
    own_dist = jnp.abs(lax.broadcasted_iota(jnp.int32, (tq, tq), 0)
                       - lax.broadcasted_iota(jnp.int32, (tq, tq), 1)).astype(F32)

    def pv(p, v, lo):
        return jnp.concatenate([_pv_with_rowsum(p[:2 * tq], v, lo),
                                _pv_with_rowsum(p[2 * tq:], v, jnp.logical_not(lo))], axis=0)

    qq_aug, carry = [], []
    for pair in range(2):
        sl = slice(pair * LANES, (pair + 1) * LANES)
        qp = q[:, sl]
        zero = jnp.zeros_like(qp)
        rows4 = jnp.concatenate([jnp.where(seg == i, qp, zero) for i in range(4)], axis=0)
        factors = jnp.concatenate([qf_ref[2 * pair]] * 2 + [qf_ref[2 * pair + 1]] * 2, axis=0)
        qq_aug.append(jnp.concatenate([rows4, factors], axis=1))
        bias = jnp.concatenate([(DIFF_SLOPES[2 * pair] * LOG2E) * own_dist] * 2
                               + [(DIFF_SLOPES[2 * pair + 1] * LOG2E) * own_dist] * 2, axis=0)
        s = _nt_dot(rows4, k_ref[0, pl.ds(q_start, tq), sl]) - bias
        m = jnp.max(s, axis=-1, keepdims=True)
        carry.append((m, pv(jnp.exp2(s - m), v_ref[0, pl.ds(q_start, tq), sl], lo_q)))

    def body(j, carry):
        k0 = pl.multiple_of(j * tk, tk)
        rel = key_k + (k0 - q_start)
        sign = jnp.where(rel < 0, 1.0, -1.0)
        kf = sign * (kf_ref[...] - jnp.where(delta_lanes, (q_start - k0).astype(F32), 0.0))
        own_key = jnp.where(rel >= 0, jnp.where(rel < tq, NEG_INF, 0.0), 0.0)
        k_factors = jnp.where(mask_lane, own_key, kf).astype(BF16)
        new = []
        for pair in range(2):
            sl = slice(pair * LANES, (pair + 1) * LANES)
            s = _nt_dot(qq_aug[pair], jnp.concatenate([k_ref[0, pl.ds(k0, tk), sl], k_factors], axis=1))
            new.append(_online_softmax_step(
                s, functools.partial(pv, v=v_ref[0, pl.ds(k0, tk), sl], lo=lo_k), carry[pair]))
        return tuple(new)

    carry = lax.fori_loop(0, seq // tk, body, tuple(carry), unroll=2)
    for pair in range(2):
        p = _normalise(carry[pair][1])
        head_a = p[0:tq] - lam * p[tq:2 * tq]
        head_b = p[2 * tq:3 * tq] - lam * p[3 * tq:4 * tq]
        o = jnp.where(lo_q, head_a, head_b)
        o = o * _pair_rms_scale(o) * subln_ref[...] * (1.0 - lambda_init)
        o_ref[0, :, pair * LANES:(pair + 1) * LANES] = o.astype(BF16)


def _diff_attention(qkv, q_factors, k_factors, lq1, lk1, lq2, lk2, subln, batch, seq, lambda_init):
    tq = DIFF_TQ
    return pl.pallas_call(
        functools.partial(_diff_kernel, seq=seq, lambda_init=lambda_init),
        grid=(batch, seq // tq),
        in_specs=[
            pl.BlockSpec((1, tq, 256), lambda b, t: (b, t, C_Q // 256)),
            pl.BlockSpec((1, seq, 256), lambda b, t: (b, 0, C_K // 256)),
            pl.BlockSpec((1, seq, 256), lambda b, t: (b, 0, C_V // 256)),
        ] + [_whole()] * 7,
        out_specs=pl.BlockSpec((1, tq, BRANCH_W), lambda b, t: (b, t, 0)),
        out_shape=jax.ShapeDtypeStruct((batch, seq, BRANCH_W), BF16),
        compiler_params=_params(2),
        name="diff_attention",
    )(qkv, qkv, qkv, q_factors, k_factors, lq1, lk1, lq2, lk2, subln)


def _swa_kernel(q_ref, k_ref, v_ref, sink_ref, o_ref, *, seq):
    tq = SWA_TQ
    span = tq + 2 * SWA_WINDOW
    q_start = pl.program_id(1) * tq
    k_start = pl.multiple_of(jnp.clip(q_start - SWA_WINDOW, 0, seq - span), SWA_WINDOW)
    q = q_ref[0]
    kw = k_ref[0, pl.ds(k_start, span), :]
    vw = v_ref[0, pl.ds(k_start, span), :]
    q_pos = q_start + lax.broadcasted_iota(jnp.int32, (tq, span), 0)
    k_pos = k_start + lax.broadcasted_iota(jnp.int32, (tq, span), 1)
    rel = jnp.abs(k_pos - q_pos)
    valid = rel <= SWA_WINDOW
    rel_f = rel.astype(F32)
    lo_q, lo_k = _lo_lanes(tq), _lo_lanes(span)
    outs = [[None, None], [None, None]]
    for kv_head in range(2):
        s_all = _nt_dot(_stack_group(q, _own(lo_q, kv_head)), kw)
        for g in range(2):
            head = 2 * kv_head + g
            sink = sink_ref[head] * LOG2E
            s = s_all[g * tq:(g + 1) * tq] - (SWA_SLOPES[head] * LOG2E) * rel_f
            s = jnp.where(valid, s, NEG_INF)
            m = jnp.maximum(jnp.max(s, axis=-1, keepdims=True), sink)
            r = _pv_with_rowsum(jnp.exp2(s - m), vw, _own(lo_k, kv_head))
            outs[g][kv_head] = r / (pltpu.roll(r, HEAD_DIM, 1) + jnp.exp2(sink - m))
    o_ref[0, :, :LANES] = jnp.where(lo_q, outs[0][0], outs[0][1]).astype(BF16)
    o_ref[0, :, LANES:] = jnp.where(lo_q, outs[1][0], outs[1][1]).astype(BF16)


def _swa_attention(qkv, sink, batch, seq):
    tq = SWA_TQ
    return pl.pallas_call(
        functools.partial(_swa_kernel, seq=seq),
        grid=(batch, seq // tq),
        in_specs=[
            pl.BlockSpec((1, tq, 256), lambda b, t: (b, t, D_Q // 256)),
            pl.BlockSpec((1, seq, LANES), lambda b, t: (b, 0, D_K // LANES)),
            pl.BlockSpec((1, seq, LANES), lambda b, t: (b, 0, D_V // LANES)),
            _whole(pltpu.SMEM),
        ],
        out_specs=pl.BlockSpec((1, tq, BRANCH_W), lambda b, t: (b, t, 0)),
        out_shape=jax.ShapeDtypeStruct((batch, seq, BRANCH_W), BF16),
        compiler_params=_params(2),
        name="swa_attention",
    )(qkv, qkv, qkv, sink)


def _merge_kernel(x_ref, oa_ref, ob_ref, oc_ref, od_ref, gpre_ref, wg_ref, wb_ref, wo_ref, gpost_ref, o_ref):
    x = _load_rows(x_ref)
    h = _rmsnorm(x, gpre_ref[...]).astype(BF16)
    merged = None
    for i, br_ref in enumerate((oa_ref, ob_ref, oc_ref, od_ref)):
        gate = jax.nn.sigmoid(jnp.dot(h, wg_ref[:, i * D_MODEL:(i + 1) * D_MODEL], preferred_element_type=F32))
        term = gate * jnp.dot(br_ref[...], wb_ref[i], preferred_element_type=F32)
        merged = term if merged is None else merged + term
    mix = jnp.dot(merged.astype(BF16), wo_ref[...], preferred_element_type=F32)
    _store_rows(o_ref, x + _rmsnorm(mix, gpost_ref[...]))


def _merge(x_cm, branches, gpre, w_gate, w_branch, w_out, gpost):
    chunked = x_cm.ndim == 3
    n_tok = x_cm.shape[-2]
    t = TOKEN_TILE
    tok = lambda w: pl.BlockSpec((t, w), lambda i: (i, 0))
    return pl.pallas_call(
        _merge_kernel,
        grid=(n_tok // t,),
        in_specs=[_row_spec(t, lambda i: i, chunked)] + [tok(BRANCH_W)] * 4 + [_whole()] * 5,
        out_specs=_row_spec(t, lambda i: i),
        out_shape=jax.ShapeDtypeStruct((D_MODEL // LANES, n_tok, LANES), F32),
        compiler_params=_params(1),
        name="merge",
    )(x_cm, *branches, gpre, w_gate, w_branch, w_out, gpost)


def _gelu_tanh(x):
    k = -2.0 * LOG2E * math.sqrt(2.0 / math.pi)
    return x / (1.0 + jnp.exp2(x * (k + (k * 0.044715) * (x * x))))


def _ffn_kernel(xp_ref, x_ref, xn_ref, gpre_ref, wup_ref, cw_ref, cb_ref, wdn_ref, gpost_ref, o_ref,
                h_scr, a_scr, *stage, tiles_per_seq):
    t = TOKEN_TILE
    n_b = t // 8
    pos = pl.program_id(0) % tiles_per_seq
    gpre = gpre_ref[...]
    n_c = D_MODEL // LANES
    x = jnp.concatenate(
        [jnp.concatenate([x_ref[c, pl.ds(b, 8, stride=n_b), :] for b in range(n_b)], axis=0) for c in range(n_c)],
        axis=1)
    h_scr[0:t, :] = _rmsnorm(x, gpre).astype(BF16)
    keep_prev = (pos != 0).astype(F32)
    keep_next = (pos != tiles_per_seq - 1).astype(F32)
    hp = pltpu.roll(_rmsnorm(_load_rows(xp_ref), gpre) * keep_prev, 1, 0)
    hn = pltpu.roll(_rmsnorm(_load_rows(xn_ref), gpre) * keep_next, 1, 0)
    hrow = lax.broadcasted_iota(jnp.int32, (HALO, D_MODEL), 0)
    h_scr[t:, :] = jnp.where(hrow == 0, hp, jnp.where(hrow == 1, hn, 0.0)).astype(BF16)
    h = h_scr[...]
    sub = lax.broadcasted_iota(jnp.int32, (8, FFN_CHUNK), 0)

    def conv(c0):
        u = jnp.dot(h, wup_ref[:, c0:c0 + FFN_CHUNK], preferred_element_type=F32)
        halo = u[t:t + 8]
        first_prev = jnp.where(sub == 0, halo[0:1], pltpu.roll(u[t - 8:t], 1, 0))
        last_next = jnp.where(sub == 7, halo[1:2], pltpu.roll(u[0:8], 7, 0))
        prev = jnp.concatenate([first_prev, u[0:t - 8]], axis=0)
        nxt = jnp.concatenate([u[8:t], last_next], axis=0)
        w = cw_ref[:, c0:c0 + FFN_CHUNK]
        return cb_ref[:, c0:c0 + FFN_CHUNK] + prev * w[0:1] + u[0:t] * w[1:2] + nxt * w[2:3]

    for ch in range(D_FF // FFN_CHUNK):
        gate = conv(ch * FFN_CHUNK)
        val = conv(D_FF + ch * FFN_CHUNK)
        a_scr[:, ch * FFN_CHUNK:(ch + 1) * FFN_CHUNK] = (_gelu_tanh(gate) * val).astype(BF16)
    y = jnp.dot(a_scr[...], wdn_ref[...], preferred_element_type=F32)
    out = x + _rmsnorm(y, gpost_ref[...])
    dst = stage[0] if stage else o_ref
    for c in range(n_c):
        for b in range(n_b):
            dst[c, pl.ds(b, 8, stride=n_b), :] = out[8 * b:8 * b + 8, c * LANES:(c + 1) * LANES]
        if stage:
            o_ref[:, c * LANES:(c + 1) * LANES] = dst[c]


def _ffn(x_cm, gpre, w_up, conv_w, conv_b, w_down, gpost, seq, chunked_out):
    n_tok = x_cm.shape[1]
    t = TOKEN_TILE
    per_tile = t // HALO
    n_halo_blocks = n_tok // HALO
    out_shape = x_cm.shape if chunked_out else (n_tok, D_MODEL)
    stage = [] if chunked_out else [pltpu.VMEM((D_MODEL // LANES, t, LANES), F32)]
    return pl.pallas_call(
        functools.partial(_ffn_kernel, tiles_per_seq=seq // t),
        grid=(n_tok // t,),
        in_specs=[
            _row_spec(HALO, lambda i: jnp.maximum(i * per_tile - 1, 0)),
            _row_spec(t, lambda i: i),
            _row_spec(HALO, lambda i: jnp.minimum((i + 1) * per_tile, n_halo_blocks - 1)),
        ] + [_whole()] * 6,
        out_specs=_row_spec(t, lambda i: i, chunked_out),
        out_shape=jax.ShapeDtypeStruct(out_shape, F32),
        scratch_shapes=[
            pltpu.VMEM((t + HALO, D_MODEL), BF16),
            pltpu.VMEM((t, D_FF), BF16),
        ] + stage,
        compiler_params=_params(1),
        name="conv_ffn",
    )(x_cm, x_cm, x_cm, gpre, w_up, conv_w, conv_b, w_down, gpost)


def _rope_tables(seq):
    t = np.arange(seq)
    axis_dim = HEAD_DIM // 2
    inv = jnp.asarray(ROPE_THETA, F32) ** (-jnp.arange(0, axis_dim, 2, dtype=F32) / axis_dim)
    ang_r = jnp.asarray(t // GRID_W, F32)[:, None] * inv
    ang_c = jnp.asarray(t % GRID_W, F32)[:, None] * inv
    cos = jnp.concatenate([jnp.cos(ang_r)] * 2 + [jnp.cos(ang_c)] * 2, axis=-1)
    sin = jnp.concatenate([-jnp.sin(ang_r), jnp.sin(ang_r), -jnp.sin(ang_c), jnp.sin(ang_c)], axis=-1)
    return jnp.tile(cos, (1, 2)), jnp.tile(sin, (1, 2))


def _group_major(w, axis):
    shape = w.shape
    w = w.reshape(shape[:axis] + (2, 2, HEAD_DIM) + shape[axis + 1:])
    return jnp.swapaxes(w, axis, axis + 1).reshape(shape)


def kernel(x_prompt, x_sample, norm_mix_pre, norm_mix_post, norm_ffn_pre, norm_ffn_post, w_in, na_rpb, gqa_q_norm, gqa_k_norm, diff_lambda_q1, diff_lambda_k1, diff_lambda_q2, diff_lambda_k2, diff_subln, swa_sink, w_branch, w_out, ffn_w_up, ffn_conv_w, ffn_conv_b, ffn_w_down):
    depth = w_in.shape[0]
    w_qkv = jnp.concatenate([
        w_in[:, :, :B_Q], _group_major(w_in[:, :, B_Q:B_K], 2), w_in[:, :, B_K:D_Q],
        _group_major(w_in[:, :, D_Q:D_K], 2), w_in[:, :, D_K:QKV_COLS]], axis=-1).astype(BF16)
    w_gate = w_in[:, :, QKV_COLS:].astype(BF16)
    w_br = w_branch.reshape(depth, N_BRANCH, BRANCH_W, D_MODEL)
    w_br = jnp.stack([w_br[:, 0], _group_major(w_br[:, 1], 1), w_br[:, 2], _group_major(w_br[:, 3], 1)],
                     axis=1).astype(BF16)
    diff_qf, diff_kf = _diff_bias_factors()
    w_o = w_out.astype(BF16)
    w_up = ffn_w_up.astype(BF16)
    w_dn = ffn_w_down.astype(BF16)
    row2 = lambda a: a.astype(F32).reshape(depth, 1, -1)
    tile2 = lambda a: jnp.tile(a.astype(F32), (1, 2)).reshape(depth, 1, -1)
    g_mix_pre, g_mix_post = row2(norm_mix_pre), row2(norm_mix_post)
    g_ffn_pre, g_ffn_post = row2(norm_ffn_pre), row2(norm_ffn_post)
    qn, kn, subln = tile2(gqa_q_norm), tile2(gqa_k_norm), tile2(diff_subln)
    lq1, lk1, lq2, lk2 = (row2(a) for a in (diff_lambda_q1, diff_lambda_k1, diff_lambda_q2, diff_lambda_k2))
    conv_b = row2(ffn_conv_b)
    conv_w = ffn_conv_w.astype(F32)
    sink = swa_sink.astype(F32)

    def run_trunk(x):
        batch, seq, _ = x.shape
        cos_t, sin_t = _rope_tables(seq)
        na_bias = _na_bias_tables(na_rpb.astype(F32), seq // GRID_W)
        x_cm = x.reshape(batch * seq, D_MODEL)
        for l in range(depth):
            lambda_init = 0.8 - 0.6 * math.exp(-0.3 * l)
            qkv = _in_proj(x_cm, g_mix_pre[l], w_qkv[l], cos_t, sin_t, qn[l], kn[l], seq)
            qkv = qkv.reshape(batch, seq, QKV_COLS)
            o_a = _na_attention(qkv, na_bias[l], batch, seq)
            o_b = _gqa_attention(qkv, batch, seq)
            o_c = _diff_attention(qkv, diff_qf, diff_kf, lq1[l], lk1[l], lq2[l], lk2[l], subln[l], batch, seq,
                                  lambda_init)
            o_d = _swa_attention(qkv, sink[l], batch, seq)
            branches = [o.reshape(batch * seq, BRANCH_W) for o in (o_a, o_b, o_c, o_d)]
            x_cm = _merge(x_cm, branches, g_mix_pre[l], w_gate[l], w_br[l], w_o[l], g_mix_post[l])
            x_cm = _ffn(x_cm, g_ffn_pre[l], w_up[l], conv_w[l], conv_b[l], w_dn[l], g_ffn_post[l], seq,
                        chunked_out=l < depth - 1)
        return x_cm.reshape(batch, seq, D_MODEL)

    return (run_trunk(x_prompt), run_trunk(x_sample))
```

```python
import functools
import math

import numpy as np
import jax
import jax.numpy as jnp
from jax import lax
from jax.experimental import pallas as pl
from jax.experimental.pallas import tpu as pltpu

F32 = jnp.float32
BF16 = jnp.bfloat16

D_MODEL = 1024
GRID_W = 64
HEAD_DIM = 64
EPS = 1e-6
NEG_INF = -1e30
LOG2E = math.log2(math.e)
NA_WIN_R = 8
NA_WIN_C = 16
ROPE_THETA = 10000.0
DIFF_QK_DIM = 32
SWA_WINDOW = 128
N_ALIBI_HEADS = 8
N_BRANCH = 4
BRANCH_W = 256
D_FF = 2816
QKV_COLS = 2560

A_Q, A_K, A_V = 0, 256, 512
B_Q, B_K, B_V = 768, 1024, 1152
C_Q, C_K, C_V = 1280, 1536, 1792
D_Q, D_K, D_V = 2048, 2304, 2432

LANES = 128
VMEM_LIMIT = 56 * 1024 * 1024

TOKEN_TILE = 512
HALO = 16
FFN_CHUNK = 256
NA_Q_ROWS = 4
NA_K_ROWS = 12
GQA_TQ, GQA_TK = 512, 2048
DIFF_TQ, DIFF_TK = 256, 2048
SWA_TQ = 256


def _alibi_slopes():
    s = 2.0 ** (-8.0 * np.arange(1, N_ALIBI_HEADS + 1) / N_ALIBI_HEADS)
    return [float(v) for v in s[0::2]], [float(v) for v in s[1::2]]


DIFF_SLOPES, SWA_SLOPES = _alibi_slopes()


def _params(n_parallel):
    return pltpu.CompilerParams(dimension_semantics=("parallel",) * n_parallel,
                                vmem_limit_bytes=VMEM_LIMIT)


def _whole(space=pltpu.VMEM):
    return pl.BlockSpec(memory_space=space)


def _row_spec(rows, index):
    return pl.BlockSpec((rows, D_MODEL), lambda i: (index(i), 0))


def _rmsnorm(x, gain):
    ms = jnp.mean(x * x, axis=-1, keepdims=True)
    return x * lax.rsqrt(ms + EPS) * gain


def _pair_rms_scale(y):
    lo = lax.broadcasted_iota(jnp.int32, y.shape, 1) < HEAD_DIM
    y2 = y * y
    s_lo = jnp.sum(jnp.where(lo, y2, 0.0), axis=-1, keepdims=True)
    s_hi = jnp.sum(jnp.where(lo, 0.0, y2), axis=-1, keepdims=True)
    ms = jnp.where(lo, s_lo, s_hi) * (1.0 / HEAD_DIM)
    return lax.rsqrt(ms + EPS)


def _nt_dot(a, b):
    return lax.dot_general(a, b, (((1,), (1,)), ((), ())), preferred_element_type=F32)


def _lo_lanes(rows):
    return lax.broadcasted_iota(jnp.int32, (rows, LANES), 1) < HEAD_DIM


def _own(lo, half):
    return lo if half == 0 else jnp.logical_not(lo)


def _pv_with_rowsum(p, v, own):
    return jnp.dot(p.astype(BF16), jnp.where(own, v, jnp.ones_like(v)), preferred_element_type=F32)


def _normalise(acc):
    return acc / pltpu.roll(acc, HEAD_DIM, 1)


def _in_proj_kernel(x_ref, gain_ref, w_ref, cos_ref, sin_ref, qn_ref, kn_ref, o_ref):
    h = _rmsnorm(x_ref[...], gain_ref[...]).astype(BF16)

    def proj(c0, width):
        return jnp.dot(h, w_ref[:, c0:c0 + width], preferred_element_type=F32)

    def plain(c0, width, scale=None):
        y = proj(c0, width)
        if scale is not None:
            y = y * scale
        o_ref[:, c0:c0 + width] = y.astype(BF16)

    def normed_rope(c0, gain_ref_, scale=None):
        y = proj(c0, LANES)
        yn = y * _pair_rms_scale(y) * gain_ref_[...]
        lane = lax.broadcasted_iota(jnp.int32, yn.shape, 1)
        partner = jnp.where((lane % 32) < 16, pltpu.roll(yn, LANES - 16, 1), pltpu.roll(yn, 16, 1))
        out = yn * cos_ref[...] + partner * sin_ref[...]
        if scale is not None:
            out = out * scale
        o_ref[:, c0:c0 + LANES] = out.astype(BF16)

    hd_scale = HEAD_DIM ** -0.5 * LOG2E
    plain(A_Q, 256, hd_scale)
    plain(A_K, 512)
    normed_rope(B_Q, qn_ref, hd_scale)
    normed_rope(B_Q + LANES, qn_ref, hd_scale)
    normed_rope(B_K, kn_ref)
    plain(B_V, 128)
    plain(C_Q, 256, DIFF_QK_DIM ** -0.5 * LOG2E)
    plain(C_K, 512)
    plain(D_Q, 256, hd_scale)
    plain(D_K, 256)


def _in_proj(x2d, gain, w_qkv, cos_t, sin_t, qn, kn, seq):
    n_tok = x2d.shape[0]
    t = TOKEN_TILE
    tiles_per_seq = seq // t
    return pl.pallas_call(
        _in_proj_kernel,
        grid=(n_tok // t,),
        in_specs=[
            _row_spec(t, lambda i: i),
            _whole(), _whole(),
            pl.BlockSpec((t, LANES), lambda i: (i % tiles_per_seq, 0)),
            pl.BlockSpec((t, LANES), lambda i: (i % tiles_per_seq, 0)),
            _whole(), _whole(),
        ],
        out_specs=pl.BlockSpec((t, QKV_COLS), lambda i: (i, 0)),
        out_shape=jax.ShapeDtypeStruct((n_tok, QKV_COLS), BF16),
        compiler_params=_params(1),
        name="in_proj",
    )(x2d, gain, w_qkv, cos_t, sin_t, qn, kn)


def _na_kernel(q_ref, k_ref, v_ref, b_ref, o_ref, *, rows):
    t = pl.program_id(1)
    tq = NA_Q_ROWS * GRID_W
    tk = NA_K_ROWS * GRID_W
    r_start = jnp.clip(t * NA_Q_ROWS - NA_WIN_R // 2, 0, rows - NA_K_ROWS)
    k0 = pl.multiple_of(r_start * GRID_W, GRID_W)
    q = q_ref[0]
    kw = k_ref[0, pl.ds(k0, tk), :]
    vw = v_ref[0, pl.ds(k0, tk), :]
    lo_q, lo_k = _lo_lanes(tq), _lo_lanes(tk)
    for pair in range(2):
        sl = slice(pair * LANES, (pair + 1) * LANES)
        qp, kp, vp = q[:, sl], kw[:, sl], vw[:, sl]
        halves = []
        for half in range(2):
            qm = jnp.where(_own(lo_q, half), qp, jnp.zeros_like(qp))
            s = _nt_dot(qm, kp) + b_ref[0, 2 * pair + half]
            e = jnp.exp2(s - jnp.max(s, axis=-1, keepdims=True))
            halves.append(_normalise(_pv_with_rowsum(e, vp, _own(lo_k, half))))
        o_ref[0, :, sl] = jnp.where(lo_q, halves[0], halves[1]).astype(BF16)


def _na_bias_tables(rpb, rows):
    depth, heads = rpb.shape[0], rpb.shape[1]
    qc, kc = np.arange(GRID_W)[:, None], np.arange(GRID_W)[None, :]
    c0 = np.clip(qc - NA_WIN_C // 2, 0, GRID_W - NA_WIN_C)
    col_valid = (kc >= c0) & (kc < c0 + NA_WIN_C)
    dc = np.clip(kc - qc + NA_WIN_C - 1, 0, 2 * NA_WIN_C - 2)
    blocks = jnp.where(col_valid, rpb[:, :, :, dc] * LOG2E, NEG_INF).astype(F32)
    masked = jnp.full((depth, heads, GRID_W, GRID_W), NEG_INF, F32)
    cases = []
    for q_row0 in (0, NA_Q_ROWS, rows - NA_Q_ROWS):
        r_start = int(np.clip(q_row0 - NA_WIN_R // 2, 0, rows - NA_K_ROWS))
        strips = []
        for i in range(NA_Q_ROWS):
            qr = q_row0 + i
            r0 = int(np.clip(qr - NA_WIN_R // 2, 0, rows - NA_WIN_R))
            strip = []
            for j in range(NA_K_ROWS):
                kr = r_start + j
                strip.append(blocks[:, :, kr - qr + NA_WIN_R - 1] if r0 <= kr < r0 + NA_WIN_R else masked)
            strips.append(jnp.concatenate(strip, axis=-1))
        cases.append(jnp.concatenate(strips, axis=-2))
    return jnp.stack(cases, axis=1)


def _na_attention(qkv, bias, batch, seq):
    rows = seq // GRID_W
    n_t = rows // NA_Q_ROWS
    tq, tk = NA_Q_ROWS * GRID_W, NA_K_ROWS * GRID_W
    case = lambda t: jnp.where(t == 0, 0, jnp.where(t == n_t - 1, 2, 1))
    return pl.pallas_call(
        functools.partial(_na_kernel, rows=rows),
        grid=(batch, n_t),
        in_specs=[
            pl.BlockSpec((1, tq, 256), lambda b, t: (b, t, A_Q // 256)),
            pl.BlockSpec((1, seq, 256), lambda b, t: (b, 0, A_K // 256)),
            pl.BlockSpec((1, seq, 256), lambda b, t: (b, 0, A_V // 256)),
            pl.BlockSpec((1, 4, tq, tk), lambda b, t: (case(t), 0, 0, 0)),
        ],
        out_specs=pl.BlockSpec((1, tq, BRANCH_W), lambda b, t: (b, t, 0)),
        out_shape=jax.ShapeDtypeStruct((batch, seq, BRANCH_W), BF16),
        compiler_params=_params(2),
        name="na_attention",
    )(qkv, qkv, qkv, bias)


def _online_softmax_step(s, pv_fn, carry):
    m, acc = carry
    m_new = jnp.maximum(m, jnp.max(s, axis=-1, keepdims=True))
    alpha = jnp.exp2(m - m_new)
    return m_new, alpha * acc + pv_fn(jnp.exp2(s - m_new))


def _softmax_init(rows):
    return (jnp.full((rows, 1), NEG_INF, F32), jnp.zeros((rows, LANES), F32))


def _stack_group(q, sel):
    g0, g1 = q[:, :LANES], q[:, LANES:]
    zero = jnp.zeros_like(g0)
    return jnp.concatenate([jnp.where(sel, g0, zero), jnp.where(sel, g1, zero)], axis=0)


def _gqa_kernel(q_ref, k_ref, v_ref, o_ref, *, seq):
    tq, tk = GQA_TQ, GQA_TK
    q = q_ref[0]
    lo_q, lo_k = _lo_lanes(tq), _lo_lanes(tk)
    qq = [_stack_group(q, _own(lo_q, h)) for h in range(2)]

    def body(j, carry):
        k0 = pl.multiple_of(j * tk, tk)
        kj = k_ref[0, pl.ds(k0, tk), :]
        vj = v_ref[0, pl.ds(k0, tk), :]
        return tuple(
            _online_softmax_step(_nt_dot(qq[h], kj),
                                 functools.partial(_pv_with_rowsum, v=vj, own=_own(lo_k, h)), carry[h])
            for h in range(2))

    res = lax.fori_loop(0, seq // tk, body, (_softmax_init(2 * tq), _softmax_init(2 * tq)), unroll=2)
    n = [_normalise(acc) for _, acc in res]
    o_ref[0, :, :LANES] = jnp.where(lo_q, n[0][:tq], n[1][:tq]).astype(BF16)
    o_ref[0, :, LANES:] = jnp.where(lo_q, n[0][tq:], n[1][tq:]).astype(BF16)


def _gqa_attention(qkv, batch, seq):
    tq = GQA_TQ
    return pl.pallas_call(
        functools.partial(_gqa_kernel, seq=seq),
        grid=(batch, seq // tq),
        in_specs=[
            pl.BlockSpec((1, tq, 256), lambda b, t: (b, t, B_Q // 256)),
            pl.BlockSpec((1, seq, LANES), lambda b, t: (b, 0, B_K // LANES)),
            pl.BlockSpec((1, seq, LANES), lambda b, t: (b, 0, B_V // LANES)),
        ],
        out_specs=pl.BlockSpec((1, tq, BRANCH_W), lambda b, t: (b, t, 0)),
        out_shape=jax.ShapeDtypeStruct((batch, seq, BRANCH_W), BF16),
        compiler_params=_params(2),
        name="gqa_attention",
    )(qkv, qkv, qkv)


_AUG_KEY_LO, _AUG_KEY_HI, _AUG_ROW, _AUG_DELTA = 0, 3, 6, 9
_AUG_MASK = 12


def _bf16_pieces(x):
    x = np.asarray(x, np.float32)
    pieces = []
    for _ in range(3):
        p = x.astype(BF16).astype(np.float32)
        pieces.append(p)
        x = x - p
    return pieces


def _diff_bias_factors():
    tq, tk = DIFF_TQ, DIFF_TK
    q_side = np.zeros((4, tq, LANES), np.float32)
    for h, slope in enumerate(DIFF_SLOPES):
        c = np.float32(slope * LOG2E)
        c3 = _bf16_pieces(c)
        r3 = _bf16_pieces(-c * np.arange(tq, dtype=np.float32))
        for n in range(3):
            for g in (_AUG_KEY_LO, _AUG_KEY_HI, _AUG_DELTA):
                q_side[h, :, g + n] = c3[n]
            q_side[h, :, _AUG_ROW + n] = r3[n]
        q_side[h, :, _AUG_MASK] = 1.0
    k_side = np.zeros((tk, LANES), np.float32)
    j = np.arange(tk)
    for n in range(3):
        k_side[:, _AUG_KEY_LO + n] = j % 256
        k_side[:, _AUG_KEY_HI + n] = 256 * (j // 256)
        k_side[:, _AUG_ROW + n] = 1.0
    return jnp.asarray(q_side, BF16), jnp.asarray(k_side, F32)


def _diff_kernel(q_ref, k_ref, v_ref, qf_ref, kf_ref, lq1_ref, lk1_ref, lq2_ref, lk2_ref, subln_ref, o_ref, *,
                 seq, lambda_init):
    tq, tk = DIFF_TQ, DIFF_TK
    q_start = pl.multiple_of(pl.program_id(1) * tq, tq)
    q = q_ref[0]
    lam = (jnp.exp(jnp.sum(lq1_ref[...] * lk1_ref[...], axis=-1, keepdims=True))
           - jnp.exp(jnp.sum(lq2_ref[...] * lk2_ref[...], axis=-1, keepdims=True)) + lambda_init)
    lane = lax.broadcasted_iota(jnp.int32, (tq, LANES), 1)
    seg = lane // DIFF_QK_DIM
    lo_q, lo_k = lane < HEAD_DIM, _lo_lanes(tk)
    lane_k = lax.broadcasted_iota(jnp.int32, (tk, LANES), 1)
    key_k = lax.broadcasted_iota(jnp.int32, (tk, LANES), 0)
    delta_lanes = (lane_k >= _AUG_DELTA) & (lane_k < _AUG_DELTA + 3)
    mask_lane = lane_k == _AUG_MASK
    own_dist = jnp.abs(lax.broadcasted_iota(jnp.int32, (tq, tq), 0)
                       - lax.broadcasted_iota(jnp.int32, (tq, tq), 1)).astype(F32)

    def pv(p, v, lo):
        return jnp.concatenate([_pv_with_rowsum(p[:2 * tq], v, lo),
                                _pv_with_rowsum(p[2 * tq:], v, jnp.logical_not(lo))], axis=0)

    qq_aug, carry = [], []
    for pair in range(2):
        sl = slice(pair * LANES, (pair + 1) * LANES)
        qp = q[:, sl]
        zero = jnp.zeros_like(qp)
        rows4 = jnp.concatenate([jnp.where(seg == i, qp, zero) for i in range(4)], axis=0)
        factors = jnp.concatenate([qf_ref[2 * pair]] * 2 + [qf_ref[2 * pair + 1]] * 2, axis=0)
        qq_aug.append(jnp.concatenate([rows4, factors], axis=1))
        bias = jnp.concatenate([(DIFF_SLOPES[2 * pair] * LOG2E) * own_dist] * 2
                               + [(DIFF_SLOPES[2 * pair + 1] * LOG2E) * own_dist] * 2, axis=0)
        s = _nt_dot(rows4, k_ref[0, pl.ds(q_start, tq), sl]) - bias
        m = jnp.max(s, axis=-1, keepdims=True)
        carry.append((m, pv(jnp.exp2(s - m), v_ref[0, pl.ds(q_start, tq), sl], lo_q)))

    def body(j, carry):
        k0 = pl.multiple_of(j * tk, tk)
        rel = key_k + (k0 - q_start)
        sign = jnp.where(rel < 0, 1.0, -1.0)
        kf = sign * (kf_ref[...] - jnp.where(delta_lanes, (q_start - k0).astype(F32), 0.0))
        own_key = jnp.where(rel >= 0, jnp.where(rel < tq, NEG_INF, 0.0), 0.0)
        k_factors = jnp.where(mask_lane, own_key, kf).astype(BF16)
        new = []
        for pair in range(2):
            sl = slice(pair * LANES, (pair + 1) * LANES)
            s = _nt_dot(qq_aug[pair], jnp.concatenate([k_ref[0, pl.ds(k0, tk), sl], k_factors], axis=1))
            new.append(_online_softmax_step(
                s, functools.partial(pv, v=v_ref[0, pl.ds(k0, tk), sl], lo=lo_k), carry[pair]))
        return tuple(new)

    carry = lax.fori_loop(0, seq // tk, body, tuple(carry), unroll=2)
    for pair in range(2):
        p = _normalise(carry[pair][1])
        head_a = p[0:tq] - lam * p[tq:2 * tq]
        head_b = p[2 * tq:3 * tq] - lam * p[3 * tq:4 * tq]
        o = jnp.where(lo_q, head_a, head_b)
        o = o * _pair_rms_scale(o) * subln_ref[...] * (1.0 - lambda_init)
        o_ref[0, :, pair * LANES:(pair + 1) * LANES] = o.astype(BF16)


def _diff_attention(qkv, q_factors, k_factors, lq1, lk1, lq2, lk2, subln, batch, seq, lambda_init):
    tq = DIFF_TQ
    return pl.pallas_call(
        functools.partial(_diff_kernel, seq=seq, lambda_init=lambda_init),
        grid=(batch, seq // tq),
        in_specs=[
            pl.BlockSpec((1, tq, 256), lambda b, t: (b, t, C_Q // 256)),
            pl.BlockSpec((1, seq, 256), lambda b, t: (b, 0, C_K // 256)),
            pl.BlockSpec((1, seq, 256), lambda b, t: (b, 0, C_V // 256)),
        ] + [_whole()] * 7,
        out_specs=pl.BlockSpec((1, tq, BRANCH_W), lambda b, t: (b, t, 0)),
        out_shape=jax.ShapeDtypeStruct((batch, seq, BRANCH_W), BF16),
        compiler_params=_params(2),
        name="diff_attention",
    )(qkv, qkv, qkv, q_factors, k_factors, lq1, lk1, lq2, lk2, subln)


def _swa_kernel(q_ref, k_ref, v_ref, sink_ref, o_ref, *, seq):
    tq = SWA_TQ
    span = tq + 2 * SWA_WINDOW
    q_start = pl.program_id(1) * tq
    k_start = pl.multiple_of(jnp.clip(q_start - SWA_WINDOW, 0, seq - span), SWA_WINDOW)
    q = q_ref[0]
    kw = k_ref[0, pl.ds(k_start, span), :]
    vw = v_ref[0, pl.ds(k_start, span), :]
    q_pos = q_start + lax.broadcasted_iota(jnp.int32, (tq, span), 0)
    k_pos = k_start + lax.broadcasted_iota(jnp.int32, (tq, span), 1)
    rel = jnp.abs(k_pos - q_pos)
    valid = rel <= SWA_WINDOW
    rel_f = rel.astype(F32)
    lo_q, lo_k = _lo_lanes(tq), _lo_lanes(span)
    outs = [[None, None], [None, None]]
    for kv_head in range(2):
        s_all = _nt_dot(_stack_group(q, _own(lo_q, kv_head)), kw)
        for g in range(2):
            head = 2 * kv_head + g
            sink = sink_ref[head] * LOG2E
            s = s_all[g * tq:(g + 1) * tq] - (SWA_SLOPES[head] * LOG2E) * rel_f
            s = jnp.where(valid, s, NEG_INF)
            m = jnp.maximum(jnp.max(s, axis=-1, keepdims=True), sink)
            r = _pv_with_rowsum(jnp.exp2(s - m), vw, _own(lo_k, kv_head))
            outs[g][kv_head] = r / (pltpu.roll(r, HEAD_DIM, 1) + jnp.exp2(sink - m))
    o_ref[0, :, :LANES] = jnp.where(lo_q, outs[0][0], outs[0][1]).astype(BF16)
    o_ref[0, :, LANES:] = jnp.where(lo_q, outs[1][0], outs[1][1]).astype(BF16)


def _swa_attention(qkv, sink, batch, seq):
    tq = SWA_TQ
    return pl.pallas_call(
        functools.partial(_swa_kernel, seq=seq),
        grid=(batch, seq // tq),
        in_specs=[
            pl.BlockSpec((1, tq, 256), lambda b, t: (b, t, D_Q // 256)),
            pl.BlockSpec((1, seq, LANES), lambda b, t: (b, 0, D_K // LANES)),
            pl.BlockSpec((1, seq, LANES), lambda b, t: (b, 0, D_V // LANES)),
            _whole(pltpu.SMEM),
        ],
        out_specs=pl.BlockSpec((1, tq, BRANCH_W), lambda b, t: (b, t, 0)),
        out_shape=jax.ShapeDtypeStruct((batch, seq, BRANCH_W), BF16),
        compiler_params=_params(2),
        name="swa_attention",
    )(qkv, qkv, qkv, sink)


def _merge_kernel(x_ref, oa_ref, ob_ref, oc_ref, od_ref, gpre_ref, wg_ref, wb_ref, wo_ref, gpost_ref, o_ref):
    x = x_ref[...]
    h = _rmsnorm(x, gpre_ref[...]).astype(BF16)
    merged = None
    for i, br_ref in enumerate((oa_ref, ob_ref, oc_ref, od_ref)):
        gate = jax.nn.sigmoid(jnp.dot(h, wg_ref[:, i * D_MODEL:(i + 1) * D_MODEL], preferred_element_type=F32))
        term = gate * jnp.dot(br_ref[...], wb_ref[i], preferred_element_type=F32)
        merged = term if merged is None else merged + term
    half = x.shape[0] // 2
    for r0 in (0, half):
        mix = jnp.dot(merged[r0:r0 + half].astype(BF16), wo_ref[...], preferred_element_type=F32)
        o_ref[r0:r0 + half, :] = x[r0:r0 + half] + _rmsnorm(mix, gpost_ref[...])


def _merge(x2d, branches, gpre, w_gate, w_branch, w_out, gpost):
    n_tok = x2d.shape[0]
    t = TOKEN_TILE
    tok = lambda w: pl.BlockSpec((t, w), lambda i: (i, 0))
    return pl.pallas_call(
        _merge_kernel,
        grid=(n_tok // t,),
        in_specs=[_row_spec(t, lambda i: i)] + [tok(BRANCH_W)] * 4 + [_whole()] * 5,
        out_specs=_row_spec(t, lambda i: i),
        out_shape=jax.ShapeDtypeStruct((n_tok, D_MODEL), F32),
        compiler_params=_params(1),
        name="merge",
    )(x2d, *branches, gpre, w_gate, w_branch, w_out, gpost)


def _gelu_tanh(x):
    k = -2.0 * LOG2E * math.sqrt(2.0 / math.pi)
    return x / (1.0 + jnp.exp2(x * (k + (k * 0.044715) * (x * x))))


def _ffn_kernel(xp_ref, x_ref, xn_ref, gpre_ref, wup_ref, cw_ref, cb_ref, wdn_ref, gpost_ref, o_ref,
                h_scr, a_scr, *, tiles_per_seq):
    t = TOKEN_TILE
    n_b = t // 8
    pos = pl.program_id(0) % tiles_per_seq
    gpre = gpre_ref[...]
    x = jnp.swapaxes(x_ref[...].reshape(8, n_b, D_MODEL), 0, 1).reshape(t, D_MODEL)
    h_scr[0:t, :] = _rmsnorm(x, gpre).astype(BF16)
    keep_prev = (pos != 0).astype(F32)
    keep_next = (pos != tiles_per_seq - 1).astype(F32)
    hp = pltpu.roll(_rmsnorm(xp_ref[...], gpre) * keep_prev, 1, 0)
    hn = pltpu.roll(_rmsnorm(xn_ref[...], gpre) * keep_next, 1, 0)
    hrow = lax.broadcasted_iota(jnp.int32, (HALO, D_MODEL), 0)
    h_scr[t:, :] = jnp.where(hrow == 0, hp, jnp.where(hrow == 1, hn, 0.0)).astype(BF16)
    h = h_scr[...]
    sub = lax.broadcasted_iota(jnp.int32, (8, FFN_CHUNK), 0)

    def conv(c0):
        u = jnp.dot(h, wup_ref[:, c0:c0 + FFN_CHUNK], preferred_element_type=F32)
        halo = u[t:t + 8]
        first_prev = jnp.where(sub == 0, halo[0:1], pltpu.roll(u[t - 8:t], 1, 0))
        last_next = jnp.where(sub == 7, halo[1:2], pltpu.roll(u[0:8], 7, 0))
        prev = jnp.concatenate([first_prev, u[0:t - 8]], axis=0)
        nxt = jnp.concatenate([u[8:t], last_next], axis=0)
        w = cw_ref[:, c0:c0 + FFN_CHUNK]
        return cb_ref[:, c0:c0 + FFN_CHUNK] + prev * w[0:1] + u[0:t] * w[1:2] + nxt * w[2:3]

    for ch in range(D_FF // FFN_CHUNK):
        gate = conv(ch * FFN_CHUNK)
        val = conv(D_FF + ch * FFN_CHUNK)
        a_scr[:, ch * FFN_CHUNK:(ch + 1) * FFN_CHUNK] = (_gelu_tanh(gate) * val).astype(BF16)
    half = t // 2
    out = jnp.concatenate(
        [x[r0:r0 + half] + _rmsnorm(jnp.dot(a_scr[r0:r0 + half, :], wdn_ref[...], preferred_element_type=F32),
                                    gpost_ref[...]) for r0 in (0, half)], axis=0)
    out = jnp.swapaxes(out.reshape(n_b, 8, D_MODEL), 0, 1).reshape(t, D_MODEL)
    o_ref[...] = out


def _ffn(x2d, gpre, w_up, conv_w, conv_b, w_down, gpost, seq):
    n_tok = x2d.shape[0]
    t = TOKEN_TILE
    per_tile = t // HALO
    n_halo_blocks = n_tok // HALO
    return pl.pallas_call(
        functools.partial(_ffn_kernel, tiles_per_seq=seq // t),
        grid=(n_tok // t,),
        in_specs=[
            _row_spec(HALO, lambda i: jnp.maximum(i * per_tile - 1, 0)),
            _row_spec(t, lambda i: i),
            _row_spec(HALO, lambda i: jnp.minimum((i + 1) * per_tile, n_halo_blocks - 1)),
        ] + [_whole()] * 6,
        out_specs=_row_spec(t, lambda i: i),
        out_shape=jax.ShapeDtypeStruct((n_tok, D_MODEL), F32),
        scratch_shapes=[
            pltpu.VMEM((t + HALO, D_MODEL), BF16),
            pltpu.VMEM((t, D_FF), BF16),
        ],
        compiler_params=_params(1),
        name="conv_ffn",
    )(x2d, x2d, x2d, gpre, w_up, conv_w, conv_b, w_down, gpost)


def _rope_tables(seq):
    t = np.arange(seq)
    axis_dim = HEAD_DIM // 2
    inv = jnp.asarray(ROPE_THETA, F32) ** (-jnp.arange(0, axis_dim, 2, dtype=F32) / axis_dim)
    ang_r = jnp.asarray(t // GRID_W, F32)[:, None] * inv
    ang_c = jnp.asarray(t % GRID_W, F32)[:, None] * inv
    cos = jnp.concatenate([jnp.cos(ang_r)] * 2 + [jnp.cos(ang_c)] * 2, axis=-1)
    sin = jnp.concatenate([-jnp.sin(ang_r), jnp.sin(ang_r), -jnp.sin(ang_c), jnp.sin(ang_c)], axis=-1)
    return jnp.tile(cos, (1, 2)), jnp.tile(sin, (1, 2))


def _group_major(w, axis):
    shape = w.shape
    w = w.reshape(shape[:axis] + (2, 2, HEAD_DIM) + shape[axis + 1:])
    return jnp.swapaxes(w, axis, axis + 1).reshape(shape)


def kernel(x_prompt, x_sample, norm_mix_pre, norm_mix_post, norm_ffn_pre, norm_ffn_post, w_in, na_rpb, gqa_q_norm, gqa_k_norm, diff_lambda_q1, diff_lambda_k1, diff_lambda_q2, diff_lambda_k2, diff_subln, swa_sink, w_branch, w_out, ffn_w_up, ffn_conv_w, ffn_conv_b, ffn_w_down):
    depth = w_in.shape[0]
    w_qkv = jnp.concatenate([
        w_in[:, :, :B_Q], _group_major(w_in[:, :, B_Q:B_K], 2), w_in[:, :, B_K:D_Q],
        _group_major(w_in[:, :, D_Q:D_K], 2), w_in[:, :, D_K:QKV_COLS]], axis=-1).astype(BF16)
    w_gate = w_in[:, :, QKV_COLS:].astype(BF16)
    w_br = w_branch.reshape(depth, N_BRANCH, BRANCH_W, D_MODEL)
    w_br = jnp.stack([w_br[:, 0], _group_major(w_br[:, 1], 1), w_br[:, 2], _group_major(w_br[:, 3], 1)],
                     axis=1).astype(BF16)
    diff_qf, diff_kf = _diff_bias_factors()
    w_o = w_out.astype(BF16)
    w_up = ffn_w_up.astype(BF16)
    w_dn = ffn_w_down.astype(BF16)
    row2 = lambda a: a.astype(F32).reshape(depth, 1, -1)
    tile2 = lambda a: jnp.tile(a.astype(F32), (1, 2)).reshape(depth, 1, -1)
    g_mix_pre, g_mix_post = row2(norm_mix_pre), row2(norm_mix_post)
    g_ffn_pre, g_ffn_post = row2(norm_ffn_pre), row2(norm_ffn_post)
    qn, kn, subln = tile2(gqa_q_norm), tile2(gqa_k_norm), tile2(diff_subln)
    lq1, lk1, lq2, lk2 = (row2(a) for a in (diff_lambda_q1, diff_lambda_k1, diff_lambda_q2, diff_lambda_k2))
    conv_b = row2(ffn_conv_b)
    conv_w = ffn_conv_w.astype(F32)
    sink = swa_sink.astype(F32)

    def run_trunk(x):
        batch, seq, _ = x.shape
        cos_t, sin_t = _rope_tables(seq)
        na_bias = _na_bias_tables(na_rpb.astype(F32), seq // GRID_W)
        x2d = x.reshape(batch * seq, D_MODEL)
        for l in range(depth):
            lambda_init = 0.8 - 0.6 * math.exp(-0.3 * l)
            qkv = _in_proj(x2d, g_mix_pre[l], w_qkv[l], cos_t, sin_t, qn[l], kn[l], seq)
            qkv = qkv.reshape(batch, seq, QKV_COLS)
            o_a = _na_attention(qkv, na_bias[l], batch, seq)
            o_b = _gqa_attention(qkv, batch, seq)
            o_c = _diff_attention(qkv, diff_qf, diff_kf, lq1[l], lk1[l], lq2[l], lk2[l], subln[l], batch, seq,
                                  lambda_init)
            o_d = _swa_attention(qkv, sink[l], batch, seq)
            branches = [o.reshape(batch * seq, BRANCH_W) for o in (o_a, o_b, o_c, o_d)]
            x2d = _merge(x2d, branches, g_mix_pre[l], w_gate[l], w_br[l], w_o[l], g_mix_post[l])
            x2d = _ffn(x2d, g_ffn_pre[l], w_up[l], conv_w[l], conv_b[l], w_dn[l], g_ffn_post[l], seq)
        return x2d.reshape(batch, seq, D_MODEL)

    return (run_trunk(x_prompt), run_trunk(x_sample))
```

```python
import functools
import math

import numpy as np
import jax
import jax.numpy as jnp
from jax import lax
from jax.experimental import pallas as pl
from jax.experimental.pallas import tpu as pltpu

F32 = jnp.float32
BF16 = jnp.bfloat16

D_MODEL = 1024
GRID_W = 64
HEAD_DIM = 64
EPS = 1e-6
NEG_INF = -1e30
LOG2E = math.log2(math.e)
NA_WIN_R = 8
NA_WIN_C = 16
ROPE_THETA = 10000.0
DIFF_QK_DIM = 32
SWA_WINDOW = 128
N_ALIBI_HEADS = 8
N_BRANCH = 4
BRANCH_W = 256
D_FF = 2816
QKV_COLS = 2560

A_Q, A_K, A_V = 0, 256, 512
B_Q, B_K, B_V = 768, 1024, 1152
C_Q, C_K, C_V = 1280, 1536, 1792
D_Q, D_K, D_V = 2048, 2304, 2432

LANES = 128
VMEM_LIMIT = 56 * 1024 * 1024

TOKEN_TILE = 512
HALO = 16
FFN_CHUNK = 256
NA_Q_ROWS = 4
NA_K_ROWS = 12
GQA_TQ, GQA_TK = 512, 2048
DIFF_TQ, DIFF_TK = 256, 2048
SWA_TQ = 256


def _alibi_slopes():
    s = 2.0 ** (-8.0 * np.arange(1, N_ALIBI_HEADS + 1) / N_ALIBI_HEADS)
    return [float(v) for v in s[0::2]], [float(v) for v in s[1::2]]


DIFF_SLOPES, SWA_SLOPES = _alibi_slopes()


def _params(n_parallel):
    return pltpu.CompilerParams(dimension_semantics=("parallel",) * n_parallel,
                                vmem_limit_bytes=VMEM_LIMIT)


def _whole(space=pltpu.VMEM):
    return pl.BlockSpec(memory_space=space)


def _row_spec(rows, index):
    return pl.BlockSpec((rows, D_MODEL), lambda i: (index(i), 0))


def _rmsnorm(x, gain):
    ms = jnp.mean(x * x, axis=-1, keepdims=True)
    return x * lax.rsqrt(ms + EPS) * gain


def _pair_rms_scale(y):
    lo = lax.broadcasted_iota(jnp.int32, y.shape, 1) < HEAD_DIM
    y2 = y * y
    s_lo = jnp.sum(jnp.where(lo, y2, 0.0), axis=-1, keepdims=True)
    s_hi = jnp.sum(jnp.where(lo, 0.0, y2), axis=-1, keepdims=True)
    ms = jnp.where(lo, s_lo, s_hi) * (1.0 / HEAD_DIM)
    return lax.rsqrt(ms + EPS)


def _nt_dot(a, b):
    return lax.dot_general(a, b, (((1,), (1,)), ((), ())), preferred_element_type=F32)


def _lo_lanes(rows):
    return lax.broadcasted_iota(jnp.int32, (rows, LANES), 1) < HEAD_DIM


def _own(lo, half):
    return lo if half == 0 else jnp.logical_not(lo)


def _pv_with_rowsum(p, v, own):
    return jnp.dot(p.astype(BF16), jnp.where(own, v, jnp.ones_like(v)), preferred_element_type=F32)


def _normalise(acc):
    return acc / pltpu.roll(acc, HEAD_DIM, 1)


def _in_proj_kernel(x_ref, gain_ref, w_ref, cos_ref, sin_ref, qn_ref, kn_ref, o_ref):
    h = _rmsnorm(x_ref[...], gain_ref[...]).astype(BF16)

    def proj(c0, width):
        return jnp.dot(h, w_ref[:, c0:c0 + width], preferred_element_type=F32)

    def plain(c0, width, scale=None):
        y = proj(c0, width)
        if scale is not None:
            y = y * scale
        o_ref[:, c0:c0 + width] = y.astype(BF16)

    def normed_rope(c0, gain_ref_, scale=None):
        y = proj(c0, LANES)
        yn = y * _pair_rms_scale(y) * gain_ref_[...]
        lane = lax.broadcasted_iota(jnp.int32, yn.shape, 1)
        partner = jnp.where((lane % 32) < 16, pltpu.roll(yn, LANES - 16, 1), pltpu.roll(yn, 16, 1))
        out = yn * cos_ref[...] + partner * sin_ref[...]
        if scale is not None:
            out = out * scale
        o_ref[:, c0:c0 + LANES] = out.astype(BF16)

    hd_scale = HEAD_DIM ** -0.5 * LOG2E
    plain(A_Q, 256, hd_scale)
    plain(A_K, 512)
    normed_rope(B_Q, qn_ref, hd_scale)
    normed_rope(B_Q + LANES, qn_ref, hd_scale)
    normed_rope(B_K, kn_ref)
    plain(B_V, 128)
    plain(C_Q, 256, DIFF_QK_DIM ** -0.5 * LOG2E)
    plain(C_K, 512)
    plain(D_Q, 256, hd_scale)
    plain(D_K, 256)


def _in_proj(x2d, gain, w_qkv, cos_t, sin_t, qn, kn, seq):
    n_tok = x2d.shape[0]
    t = TOKEN_TILE
    tiles_per_seq = seq // t
    return pl.pallas_call(
        _in_proj_kernel,
        grid=(n_tok // t,),
        in_specs=[
            _row_spec(t, lambda i: i),
            _whole(), _whole(),
            pl.BlockSpec((t, LANES), lambda i: (i % tiles_per_seq, 0)),
            pl.BlockSpec((t, LANES), lambda i: (i % tiles_per_seq, 0)),
            _whole(), _whole(),
        ],
        out_specs=pl.BlockSpec((t, QKV_COLS), lambda i: (i, 0)),
        out_shape=jax.ShapeDtypeStruct((n_tok, QKV_COLS), BF16),
        compiler_params=_params(1),
        name="in_proj",
    )(x2d, gain, w_qkv, cos_t, sin_t, qn, kn)


def _na_kernel(q_ref, k_ref, v_ref, b_ref, o_ref, *, rows):
    t = pl.program_id(1)
    tq = NA_Q_ROWS * GRID_W
    tk = NA_K_ROWS * GRID_W
    r_start = jnp.clip(t * NA_Q_ROWS - NA_WIN_R // 2, 0, rows - NA_K_ROWS)
    k0 = pl.multiple_of(r_start * GRID_W, GRID_W)
    q = q_ref[0]
    kw = k_ref[0, pl.ds(k0, tk), :]
    vw = v_ref[0, pl.ds(k0, tk), :]
    lo_q, lo_k = _lo_lanes(tq), _lo_lanes(tk)
    for pair in range(2):
        sl = slice(pair * LANES, (pair + 1) * LANES)
        qp, kp, vp = q[:, sl], kw[:, sl], vw[:, sl]
        zero = jnp.zeros_like(qp)
        s_all = _nt_dot(jnp.concatenate([jnp.where(lo_q, qp, zero), jnp.where(lo_q, zero, qp)], axis=0), kp)
        halves = []
        for half in range(2):
            s = s_all[half * tq:(half + 1) * tq] + b_ref[0, 2 * pair + half]
            e = jnp.exp2(s - jnp.max(s, axis=-1, keepdims=True))
            halves.append(_normalise(_pv_with_rowsum(e, vp, _own(lo_k, half))))
        o_ref[0, :, sl] = jnp.where(lo_q, halves[0], halves[1]).astype(BF16)


def _na_bias_tables(rpb, rows):
    depth, heads = rpb.shape[0], rpb.shape[1]
    qc, kc = np.arange(GRID_W)[:, None], np.arange(GRID_W)[None, :]
    c0 = np.clip(qc - NA_WIN_C // 2, 0, GRID_W - NA_WIN_C)
    col_valid = (kc >= c0) & (kc < c0 + NA_WIN_C)
    dc = np.clip(kc - qc + NA_WIN_C - 1, 0, 2 * NA_WIN_C - 2)
    blocks = jnp.where(col_valid, rpb[:, :, :, dc] * LOG2E, NEG_INF).astype(F32)
    masked = jnp.full((depth, heads, GRID_W, GRID_W), NEG_INF, F32)
    cases = []
    for q_row0 in (0, NA_Q_ROWS, rows - NA_Q_ROWS):
        r_start = int(np.clip(q_row0 - NA_WIN_R // 2, 0, rows - NA_K_ROWS))
        strips = []
        for i in range(NA_Q_ROWS):
            qr = q_row0 + i
            r0 = int(np.clip(qr - NA_WIN_R // 2, 0, rows - NA_WIN_R))
            strip = []
            for j in range(NA_K_ROWS):
                kr = r_start + j
                strip.append(blocks[:, :, kr - qr + NA_WIN_R - 1] if r0 <= kr < r0 + NA_WIN_R else masked)
            strips.append(jnp.concatenate(strip, axis=-1))
        cases.append(jnp.concatenate(strips, axis=-2))
    return jnp.stack(cases, axis=1)


def _na_attention(qkv, bias, batch, seq):
    rows = seq // GRID_W
    n_t = rows // NA_Q_ROWS
    tq, tk = NA_Q_ROWS * GRID_W, NA_K_ROWS * GRID_W
    case = lambda t: jnp.where(t == 0, 0, jnp.where(t == n_t - 1, 2, 1))
    return pl.pallas_call(
        functools.partial(_na_kernel, rows=rows),
        grid=(batch, n_t),
        in_specs=[
            pl.BlockSpec((1, tq, 256), lambda b, t: (b, t, A_Q // 256)),
            pl.BlockSpec((1, seq, 256), lambda b, t: (b, 0, A_K // 256)),
            pl.BlockSpec((1, seq, 256), lambda b, t: (b, 0, A_V // 256)),
            pl.BlockSpec((1, 4, tq, tk), lambda b, t: (case(t), 0, 0, 0)),
        ],
        out_specs=pl.BlockSpec((1, tq, BRANCH_W), lambda b, t: (b, t, 0)),
        out_shape=jax.ShapeDtypeStruct((batch, seq, BRANCH_W), BF16),
        compiler_params=_params(2),
        name="na_attention",
    )(qkv, qkv, qkv, bias)


def _online_softmax_step(s, pv_fn, carry):
    m, acc = carry
    m_new = jnp.maximum(m, jnp.max(s, axis=-1, keepdims=True))
    alpha = jnp.exp2(m - m_new)
    return m_new, alpha * acc + pv_fn(jnp.exp2(s - m_new))


def _softmax_init(rows):
    return (jnp.full((rows, 1), NEG_INF, F32), jnp.zeros((rows, LANES), F32))


def _stack_group(q, sel):
    g0, g1 = q[:, :LANES], q[:, LANES:]
    zero = jnp.zeros_like(g0)
    return jnp.concatenate([jnp.where(sel, g0, zero), jnp.where(sel, g1, zero)], axis=0)


def _gqa_kernel(q_ref, k_ref, v_ref, o_ref, *, seq):
    tq, tk = GQA_TQ, GQA_TK
    q = q_ref[0]
    lo_q, lo_k = _lo_lanes(tq), _lo_lanes(tk)
    qq = [_stack_group(q, _own(lo_q, h)) for h in range(2)]

    def body(j, carry):
        k0 = pl.multiple_of(j * tk, tk)
        kj = k_ref[0, pl.ds(k0, tk), :]
        vj = v_ref[0, pl.ds(k0, tk), :]
        return tuple(
            _online_softmax_step(_nt_dot(qq[h], kj),
                                 functools.partial(_pv_with_rowsum, v=vj, own=_own(lo_k, h)), carry[h])
            for h in range(2))

    res = lax.fori_loop(0, seq // tk, body, (_softmax_init(2 * tq), _softmax_init(2 * tq)), unroll=2)
    n = [_normalise(acc) for _, acc in res]
    o_ref[0, :, :LANES] = jnp.where(lo_q, n[0][:tq], n[1][:tq]).astype(BF16)
    o_ref[0, :, LANES:] = jnp.where(lo_q, n[0][tq:], n[1][tq:]).astype(BF16)


def _gqa_attention(qkv, batch, seq):
    tq = GQA_TQ
    return pl.pallas_call(
        functools.partial(_gqa_kernel, seq=seq),
        grid=(batch, seq // tq),
        in_specs=[
            pl.BlockSpec((1, tq, 256), lambda b, t: (b, t, B_Q // 256)),
            pl.BlockSpec((1, seq, LANES), lambda b, t: (b, 0, B_K // LANES)),
            pl.BlockSpec((1, seq, LANES), lambda b, t: (b, 0, B_V // LANES)),
        ],
        out_specs=pl.BlockSpec((1, tq, BRANCH_W), lambda b, t: (b, t, 0)),
        out_shape=jax.ShapeDtypeStruct((batch, seq, BRANCH_W), BF16),
        compiler_params=_params(2),
        name="gqa_attention",
    )(qkv, qkv, qkv)


_AUG_KEY_LO, _AUG_KEY_HI, _AUG_ROW, _AUG_DELTA = 0, 3, 6, 9
_AUG_MASK = 12


def _bf16_pieces(x):
    x = np.asarray(x, np.float32)
    pieces = []
    for _ in range(3):
        p = x.astype(BF16).astype(np.float32)
        pieces.append(p)
        x = x - p
    return pieces


def _diff_bias_factors():
    tq, tk = DIFF_TQ, DIFF_TK
    q_side = np.zeros((4, tq, LANES), np.float32)
    for h, slope in enumerate(DIFF_SLOPES):
        c = np.float32(slope * LOG2E)
        c3 = _bf16_pieces(c)
        r3 = _bf16_pieces(-c * np.arange(tq, dtype=np.float32))
        for n in range(3):
            for g in (_AUG_KEY_LO, _AUG_KEY_HI, _AUG_DELTA):
                q_side[h, :, g + n] = c3[n]
            q_side[h, :, _AUG_ROW + n] = r3[n]
        q_side[h, :, _AUG_MASK] = 1.0
    k_side = np.zeros((tk, LANES), np.float32)
    j = np.arange(tk)
    for n in range(3):
        k_side[:, _AUG_KEY_LO + n] = j % 256
        k_side[:, _AUG_KEY_HI + n] = 256 * (j // 256)
        k_side[:, _AUG_ROW + n] = 1.0
    return jnp.asarray(q_side, BF16), jnp.asarray(k_side, F32)


def _diff_kernel(q_ref, k_ref, v_ref, qf_ref, kf_ref, lq1_ref, lk1_ref, lq2_ref, lk2_ref, subln_ref, o_ref, *,
                 seq, lambda_init):
    tq, tk = DIFF_TQ, DIFF_TK
    q_start = pl.multiple_of(pl.program_id(1) * tq, tq)
    q = q_ref[0]
    lam = (jnp.exp(jnp.sum(lq1_ref[...] * lk1_ref[...], axis=-1, keepdims=True))
           - jnp.exp(jnp.sum(lq2_ref[...] * lk2_ref[...], axis=-1, keepdims=True)) + lambda_init)
    lane = lax.broadcasted_iota(jnp.int32, (tq, LANES), 1)
    seg = lane // DIFF_QK_DIM
    lo_q, lo_k = lane < HEAD_DIM, _lo_lanes(tk)
    lane_k = lax.broadcasted_iota(jnp.int32, (tk, LANES), 1)
    key_k = lax.broadcasted_iota(jnp.int32, (tk, LANES), 0)
    delta_lanes = (lane_k >= _AUG_DELTA) & (lane_k < _AUG_DELTA + 3)
    mask_lane = lane_k == _AUG_MASK
    own_dist = jnp.abs(lax.broadcasted_iota(jnp.int32, (tq, tq), 0)
                       - lax.broadcasted_iota(jnp.int32, (tq, tq), 1)).astype(F32)

    def pv(p, v, lo):
        return jnp.concatenate([_pv_with_rowsum(p[:2 * tq], v, lo),
                                _pv_with_rowsum(p[2 * tq:], v, jnp.logical_not(lo))], axis=0)

    qq_aug, carry = [], []
    for pair in range(2):
        sl = slice(pair * LANES, (pair + 1) * LANES)
        qp = q[:, sl]
        zero = jnp.zeros_like(qp)
        rows4 = jnp.concatenate([jnp.where(seg == i, qp, zero) for i in range(4)], axis=0)
        factors = jnp.concatenate([qf_ref[2 * pair]] * 2 + [qf_ref[2 * pair + 1]] * 2, axis=0)
        qq_aug.append(jnp.concatenate([rows4, factors], axis=1))
        bias = jnp.concatenate([(DIFF_SLOPES[2 * pair] * LOG2E) * own_dist] * 2
                               + [(DIFF_SLOPES[2 * pair + 1] * LOG2E) * own_dist] * 2, axis=0)
        s = _nt_dot(rows4, k_ref[0, pl.ds(q_start, tq), sl]) - bias
        m = jnp.max(s, axis=-1, keepdims=True)
        carry.append((m, pv(jnp.exp2(s - m), v_ref[0, pl.ds(q_start, tq), sl], lo_q)))

    def body(j, carry):
        k0 = pl.multiple_of(j * tk, tk)
        rel = key_k + (k0 - q_start)
        sign = jnp.where(rel < 0, 1.0, -1.0)
        kf = sign * (kf_ref[...] - jnp.where(delta_lanes, (q_start - k0).astype(F32), 0.0))
        own_key = jnp.where(rel >= 0, jnp.where(rel < tq, NEG_INF, 0.0), 0.0)
        k_factors = jnp.where(mask_lane, own_key, kf).astype(BF16)
        new = []
        for pair in range(2):
            sl = slice(pair * LANES, (pair + 1) * LANES)
            s = _nt_dot(qq_aug[pair], jnp.concatenate([k_ref[0, pl.ds(k0, tk), sl], k_factors], axis=1))
            new.append(_online_softmax_step(
                s, functools.partial(pv, v=v_ref[0, pl.ds(k0, tk), sl], lo=lo_k), carry[pair]))
        return tuple(new)

    carry = lax.fori_loop(0, seq // tk, body, tuple(carry), unroll=2)
    for pair in range(2):
        p = _normalise(carry[pair][1])
        head_a = p[0:tq] - lam * p[tq:2 * tq]
        head_b = p[2 * tq:3 * tq] - lam * p[3 * tq:4 * tq]
        o = jnp.where(lo_q, head_a, head_b)
        o = o * _pair_rms_scale(o) * subln_ref[...] * (1.0 - lambda_init)
        o_ref[0, :, pair * LANES:(pair + 1) * LANES] = o.astype(BF16)


def _diff_attention(qkv, q_factors, k_factors, lq1, lk1, lq2, lk2, subln, batch, seq, lambda_init):
    tq = DIFF_TQ
    return pl.pallas_call(
        functools.partial(_diff_kernel, seq=seq, lambda_init=lambda_init),
        grid=(batch, seq // tq),
        in_specs=[
            pl.BlockSpec((1, tq, 256), lambda b, t: (b, t, C_Q // 256)),
            pl.BlockSpec((1, seq, 256), lambda b, t: (b, 0, C_K // 256)),
            pl.BlockSpec((1, seq, 256), lambda b, t: (b, 0, C_V // 256)),
        ] + [_whole()] * 7,
        out_specs=pl.BlockSpec((1, tq, BRANCH_W), lambda b, t: (b, t, 0)),
        out_shape=jax.ShapeDtypeStruct((batch, seq, BRANCH_W), BF16),
        compiler_params=_params(2),
        name="diff_attention",
    )(qkv, qkv, qkv, q_factors, k_factors, lq1, lk1, lq2, lk2, subln)


def _swa_kernel(q_ref, k_ref, v_ref, b_ref, sink_ref, o_ref, *, seq):
    tq = SWA_TQ
    span = tq + 2 * SWA_WINDOW
    q_start = pl.program_id(1) * tq
    k_start = pl.multiple_of(jnp.clip(q_start - SWA_WINDOW, 0, seq - span), SWA_WINDOW)
    q = q_ref[0]
    kw = k_ref[0, pl.ds(k_start, span), :]
    vw = v_ref[0, pl.ds(k_start, span), :]
    lo_q, lo_k = _lo_lanes(tq), _lo_lanes(span)
    outs = [[None, None], [None, None]]
    for kv_head in range(2):
        s_all = _nt_dot(_stack_group(q, _own(lo_q, kv_head)), kw)
        for g in range(2):
            head = 2 * kv_head + g
            sink = sink_ref[head] * LOG2E
            s = s_all[g * tq:(g + 1) * tq] + b_ref[0, head]
            m = jnp.maximum(jnp.max(s, axis=-1, keepdims=True), sink)
            r = _pv_with_rowsum(jnp.exp2(s - m), vw, _own(lo_k, kv_head))
            outs[g][kv_head] = r / (pltpu.roll(r, HEAD_DIM, 1) + jnp.exp2(sink - m))
    o_ref[0, :, :LANES] = jnp.where(lo_q, outs[0][0], outs[0][1]).astype(BF16)
    o_ref[0, :, LANES:] = jnp.where(lo_q, outs[1][0], outs[1][1]).astype(BF16)


def _swa_bias_tables():
    tq = SWA_TQ
    span = tq + 2 * SWA_WINDOW
    i, j = np.arange(tq)[:, None], np.arange(span)[None, :]
    cases = []
    for lead in (0, SWA_WINDOW, 2 * SWA_WINDOW):
        rel = np.abs(j - i - lead)
        cases.append(np.stack([np.where(rel <= SWA_WINDOW, -(slope * LOG2E) * rel, NEG_INF) for slope in SWA_SLOPES]))
    return jnp.asarray(np.stack(cases), F32)


def _swa_attention(qkv, sink, batch, seq):
    tq = SWA_TQ
    span = tq + 2 * SWA_WINDOW
    n_t = seq // tq
    case = lambda t: jnp.where(t == 0, 0, jnp.where(t == n_t - 1, 2, 1))
    return pl.pallas_call(
        functools.partial(_swa_kernel, seq=seq),
        grid=(batch, n_t),
        in_specs=[
            pl.BlockSpec((1, tq, 256), lambda b, t: (b, t, D_Q // 256)),
            pl.BlockSpec((1, seq, LANES), lambda b, t: (b, 0, D_K // LANES)),
            pl.BlockSpec((1, seq, LANES), lambda b, t: (b, 0, D_V // LANES)),
            pl.BlockSpec((1, 4, tq, span), lambda b, t: (case(t), 0, 0, 0)),
            _whole(pltpu.SMEM),
        ],
        out_specs=pl.BlockSpec((1, tq, BRANCH_W), lambda b, t: (b, t, 0)),
        out_shape=jax.ShapeDtypeStruct((batch, seq, BRANCH_W), BF16),
        compiler_params=_params(2),
        name="swa_attention",
    )(qkv, qkv, qkv, _swa_bias_tables(), sink)


def _merge_kernel(x_ref, oa_ref, ob_ref, oc_ref, od_ref, gpre_ref, wg_ref, wb_ref, wo_ref, gpost_ref, o_ref):
    x = x_ref[...]
    h = _rmsnorm(x, gpre_ref[...]).astype(BF16)
    merged = None
    for i, br_ref in enumerate((oa_ref, ob_ref, oc_ref, od_ref)):
        gate = jax.nn.sigmoid(jnp.dot(h, wg_ref[:, i * D_MODEL:(i + 1) * D_MODEL], preferred_element_type=F32))
        term = gate * jnp.dot(br_ref[...], wb_ref[i], preferred_element_type=F32)
        merged = term if merged is None else merged + term
    half = x.shape[0] // 2
    for r0 in (0, half):
        mix = jnp.dot(merged[r0:r0 + half].astype(BF16), wo_ref[...], preferred_element_type=F32)
        o_ref[r0:r0 + half, :] = x[r0:r0 + half] + _rmsnorm(mix, gpost_ref[...])


def _merge(x2d, branches, gpre, w_gate, w_branch, w_out, gpost):
    n_tok = x2d.shape[0]
    t = TOKEN_TILE
    tok = lambda w: pl.BlockSpec((t, w), lambda i: (i, 0))
    return pl.pallas_call(
        _merge_kernel,
        grid=(n_tok // t,),
        in_specs=[_row_spec(t, lambda i: i)] + [tok(BRANCH_W)] * 4 + [_whole()] * 5,
        out_specs=_row_spec(t, lambda i: i),
        out_shape=jax.ShapeDtypeStruct((n_tok, D_MODEL), F32),
        compiler_params=_params(1),
        name="merge",
    )(x2d, *branches, gpre, w_gate, w_branch, w_out, gpost)


def _gelu_tanh(x):
    k = -2.0 * LOG2E * math.sqrt(2.0 / math.pi)
    return x / (1.0 + jnp.exp2(x * (k + (k * 0.044715) * (x * x))))


def _ffn_kernel(xp_ref, x_ref, xn_ref, gpre_ref, wup_ref, cw_ref, cb_ref, wdn_ref, gpost_ref, o_ref,
                h_scr, a_scr, *, tiles_per_seq):
    t = TOKEN_TILE
    n_b = t // 8
    pos = pl.program_id(0) % tiles_per_seq
    gpre = gpre_ref[...]
    x = jnp.swapaxes(x_ref[...].reshape(8, n_b, D_MODEL), 0, 1).reshape(t, D_MODEL)
    h_scr[0:t, :] = _rmsnorm(x, gpre).astype(BF16)
    keep_prev = (pos != 0).astype(F32)
    keep_next = (pos != tiles_per_seq - 1).astype(F32)
    hp = pltpu.roll(_rmsnorm(xp_ref[...], gpre) * keep_prev, 1, 0)
    hn = pltpu.roll(_rmsnorm(xn_ref[...], gpre) * keep_next, 1, 0)
    hrow = lax.broadcasted_iota(jnp.int32, (HALO, D_MODEL), 0)
    h_scr[t:, :] = jnp.where(hrow == 0, hp, jnp.where(hrow == 1, hn, 0.0)).astype(BF16)
    h = h_scr[...]
    sub = lax.broadcasted_iota(jnp.int32, (8, FFN_CHUNK), 0)

    def conv(c0):
        u = jnp.dot(h, wup_ref[:, c0:c0 + FFN_CHUNK], preferred_element_type=F32)
        halo = u[t:t + 8]
        first_prev = jnp.where(sub == 0, halo[0:1], pltpu.roll(u[t - 8:t], 1, 0))
        last_next = jnp.where(sub == 7, halo[1:2], pltpu.roll(u[0:8], 7, 0))
        prev = jnp.concatenate([first_prev, u[0:t - 8]], axis=0)
        nxt = jnp.concatenate([u[8:t], last_next], axis=0)
        w = cw_ref[:, c0:c0 + FFN_CHUNK]
        return cb_ref[:, c0:c0 + FFN_CHUNK] + prev * w[0:1] + u[0:t] * w[1:2] + nxt * w[2:3]

    for ch in range(D_FF // FFN_CHUNK):
        gate = conv(ch * FFN_CHUNK)
        val = conv(D_FF + ch * FFN_CHUNK)
        a_scr[:, ch * FFN_CHUNK:(ch + 1) * FFN_CHUNK] = (_gelu_tanh(gate) * val).astype(BF16)
    half = t // 2
    out = jnp.concatenate(
        [x[r0:r0 + half] + _rmsnorm(jnp.dot(a_scr[r0:r0 + half, :], wdn_ref[...], preferred_element_type=F32),
                                    gpost_ref[...]) for r0 in (0, half)], axis=0)
    out = jnp.swapaxes(out.reshape(n_b, 8, D_MODEL), 0, 1).reshape(t, D_MODEL)
    o_ref[...] = out


def _ffn(x2d, gpre, w_up, conv_w, conv_b, w_down, gpost, seq):
    n_tok = x2d.shape[0]
    t = TOKEN_TILE
    per_tile = t // HALO
    n_halo_blocks = n_tok // HALO
    return pl.pallas_call(
        functools.partial(_ffn_kernel, tiles_per_seq=seq // t),
        grid=(n_tok // t,),
        in_specs=[
            _row_spec(HALO, lambda i: jnp.maximum(i * per_tile - 1, 0)),
            _row_spec(t, lambda i: i),
            _row_spec(HALO, lambda i: jnp.minimum((i + 1) * per_tile, n_halo_blocks - 1)),
        ] + [_whole()] * 6,
        out_specs=_row_spec(t, lambda i: i),
        out_shape=jax.ShapeDtypeStruct((n_tok, D_MODEL), F32),
        scratch_shapes=[
            pltpu.VMEM((t + HALO, D_MODEL), BF16),
            pltpu.VMEM((t, D_FF), BF16),
        ],
        compiler_params=_params(1),
        name="conv_ffn",
    )(x2d, x2d, x2d, gpre, w_up, conv_w, conv_b, w_down, gpost)


def _rope_tables(seq):
    t = np.arange(seq)
    axis_dim = HEAD_DIM // 2
    inv = jnp.asarray(ROPE_THETA, F32) ** (-jnp.arange(0, axis_dim, 2, dtype=F32) / axis_dim)
    ang_r = jnp.asarray(t // GRID_W, F32)[:, None] * inv
    ang_c = jnp.asarray(t % GRID_W, F32)[:, None] * inv
    cos = jnp.concatenate([jnp.cos(ang_r)] * 2 + [jnp.cos(ang_c)] * 2, axis=-1)
    sin = jnp.concatenate([-jnp.sin(ang_r), jnp.sin(ang_r), -jnp.sin(ang_c), jnp.sin(ang_c)], axis=-1)
    return jnp.tile(cos, (1, 2)), jnp.tile(sin, (1, 2))


def _group_major(w, axis):
    shape = w.shape
    w = w.reshape(shape[:axis] + (2, 2, HEAD_DIM) + shape[axis + 1:])
    return jnp.swapaxes(w, axis, axis + 1).reshape(shape)


def kernel(x_prompt, x_sample, norm_mix_pre, norm_mix_post, norm_ffn_pre, norm_ffn_post, w_in, na_rpb, gqa_q_norm, gqa_k_norm, diff_lambda_q1, diff_lambda_k1, diff_lambda_q2, diff_lambda_k2, diff_subln, swa_sink, w_branch, w_out, ffn_w_up, ffn_conv_w, ffn_conv_b, ffn_w_down):
    depth = w_in.shape[0]
    w_qkv = jnp.concatenate([
        w_in[:, :, :B_Q], _group_major(w_in[:, :, B_Q:B_K], 2), w_in[:, :, B_K:D_Q],
        _group_major(w_in[:, :, D_Q:D_K], 2), w_in[:, :, D_K:QKV_COLS]], axis=-1).astype(BF16)
    w_gate = w_in[:, :, QKV_COLS:].astype(BF16)
    w_br = w_branch.reshape(depth, N_BRANCH, BRANCH_W, D_MODEL)
    w_br = jnp.stack([w_br[:, 0], _group_major(w_br[:, 1], 1), w_br[:, 2], _group_major(w_br[:, 3], 1)],
                     axis=1).astype(BF16)
    diff_qf, diff_kf = _diff_bias_factors()
    w_o = w_out.astype(BF16)
    w_up = ffn_w_up.astype(BF16)
    w_dn = ffn_w_down.astype(BF16)
    row2 = lambda a: a.astype(F32).reshape(depth, 1, -1)
    tile2 = lambda a: jnp.tile(a.astype(F32), (1, 2)).reshape(depth, 1, -1)
    g_mix_pre, g_mix_post = row2(norm_mix_pre), row2(norm_mix_post)
    g_ffn_pre, g_ffn_post = row2(norm_ffn_pre), row2(norm_ffn_post)
    qn, kn, subln = tile2(gqa_q_norm), tile2(gqa_k_norm), tile2(diff_subln)
    lq1, lk1, lq2, lk2 = (row2(a) for a in (diff_lambda_q1, diff_lambda_k1, diff_lambda_q2, diff_lambda_k2))
    conv_b = row2(ffn_conv_b)
    conv_w = ffn_conv_w.astype(F32)
    sink = swa_sink.astype(F32)

    def run_trunk(x):
        batch, seq, _ = x.shape
        cos_t, sin_t = _rope_tables(seq)
        na_bias = _na_bias_tables(na_rpb.astype(F32), seq // GRID_W)
        x2d = x.reshape(batch * seq, D_MODEL)
        for l in range(depth):
            lambda_init = 0.8 - 0.6 * math.exp(-0.3 * l)
            qkv = _in_proj(x2d, g_mix_pre[l], w_qkv[l], cos_t, sin_t, qn[l], kn[l], seq)
            qkv = qkv.reshape(batch, seq, QKV_COLS)
            o_a = _na_attention(qkv, na_bias[l], batch, seq)
            o_b = _gqa_attention(qkv, batch, seq)
            o_c = _diff_attention(qkv, diff_qf, diff_kf, lq1[l], lk1[l], lq2[l], lk2[l], subln[l], batch, seq,
                                  lambda_init)
            o_d = _swa_attention(qkv, sink[l], batch, seq)
            branches = [o.reshape(batch * seq, BRANCH_W) for o in (o_a, o_b, o_c, o_d)]
            x2d = _merge(x2d, branches, g_mix_pre[l], w_gate[l], w_br[l], w_o[l], g_mix_post[l])
            x2d = _ffn(x2d, g_ffn_pre[l], w_up[l], conv_w[l], conv_b[l], w_dn[l], g_ffn_post[l], seq)
        return x2d.reshape(batch, seq, D_MODEL)

    return (run_trunk(x_prompt), run_trunk(x_sample))
```

```python
import functools
import math

import numpy as np
import jax
import jax.numpy as jnp
from jax import lax
from jax.experimental import pallas as pl
from jax.experimental.pallas import tpu as pltpu

F32 = jnp.float32
BF16 = jnp.bfloat16

D_MODEL = 1024
GRID_W = 64
HEAD_DIM = 64
EPS = 1e-6
NEG_INF = -1e30
LOG2E = math.log2(math.e)
NA_WIN_R = 8
NA_WIN_C = 16
ROPE_THETA = 10000.0
DIFF_QK_DIM = 32
SWA_WINDOW = 128
N_ALIBI_HEADS = 8
N_BRANCH = 4
BRANCH_W = 256
D_FF = 2816
QKV_COLS = 2560

A_Q, A_K, A_V = 0, 256, 512
B_Q, B_K, B_V = 768, 1024, 1152
C_Q, C_K, C_V = 1280, 1536, 1792
D_Q, D_K, D_V = 2048, 2304, 2432

LANES = 128
VMEM_LIMIT = 56 * 1024 * 1024

TOKEN_TILE = 1024
HALO = 16
FFN_CHUNK = 256
NA_Q_ROWS = 4
NA_K_ROWS = 12
GQA_TQ, GQA_TK = 512, 2048
DIFF_TQ, DIFF_TK = 256, 2048
SWA_TQ = 256


def _alibi_slopes():
    s = 2.0 ** (-8.0 * np.arange(1, N_ALIBI_HEADS + 1) / N_ALIBI_HEADS)
    return [float(v) for v in s[0::2]], [float(v) for v in s[1::2]]


DIFF_SLOPES, SWA_SLOPES = _alibi_slopes()


def _params(n_parallel):
    return pltpu.CompilerParams(dimension_semantics=("parallel",) * n_parallel,
                                vmem_limit_bytes=VMEM_LIMIT)


def _whole(space=pltpu.VMEM):
    return pl.BlockSpec(memory_space=space)


def _row_spec(rows, index):
    return pl.BlockSpec((rows, D_MODEL), lambda i: (index(i), 0))


def _rmsnorm(x, gain):
    ms = jnp.mean(x * x, axis=-1, keepdims=True)
    return x * lax.rsqrt(ms + EPS) * gain


def _pair_rms_scale(y):
    lo = lax.broadcasted_iota(jnp.int32, y.shape, 1) < HEAD_DIM
    y2 = y * y
    s_lo = jnp.sum(jnp.where(lo, y2, 0.0), axis=-1, keepdims=True)
    s_hi = jnp.sum(jnp.where(lo, 0.0, y2), axis=-1, keepdims=True)
    ms = jnp.where(lo, s_lo, s_hi) * (1.0 / HEAD_DIM)
    return lax.rsqrt(ms + EPS)


def _nt_dot(a, b):
    return lax.dot_general(a, b, (((1,), (1,)), ((), ())), preferred_element_type=F32)


def _lo_lanes(rows):
    return lax.broadcasted_iota(jnp.int32, (rows, LANES), 1) < HEAD_DIM


def _own(lo, half):
    return lo if half == 0 else jnp.logical_not(lo)


def _pv_with_rowsum(p, v, own):
    return jnp.dot(p.astype(BF16), jnp.where(own, v, jnp.ones_like(v)), preferred_element_type=F32)


def _normalise(acc):
    return acc / pltpu.roll(acc, HEAD_DIM, 1)


def _in_proj_kernel(x_ref, gain_ref, w_ref, cos_ref, sin_ref, qn_ref, kn_ref, o_ref):
    h = _rmsnorm(x_ref[...], gain_ref[...]).astype(BF16)

    def proj(c0, width):
        return jnp.dot(h, w_ref[:, c0:c0 + width], preferred_element_type=F32)

    def plain(c0, width, scale=None):
        y = proj(c0, width)
        if scale is not None:
            y = y * scale
        o_ref[:, c0:c0 + width] = y.astype(BF16)

    def normed_rope(c0, gain_ref_, scale=None):
        y = proj(c0, LANES)
        yn = y * _pair_rms_scale(y) * gain_ref_[...]
        lane = lax.broadcasted_iota(jnp.int32, yn.shape, 1)
        partner = jnp.where((lane % 32) < 16, pltpu.roll(yn, LANES - 16, 1), pltpu.roll(yn, 16, 1))
        out = yn * cos_ref[...] + partner * sin_ref[...]
        if scale is not None:
            out = out * scale
        o_ref[:, c0:c0 + LANES] = out.astype(BF16)

    hd_scale = HEAD_DIM ** -0.5 * LOG2E
    plain(A_Q, 256, hd_scale)
    plain(A_K, 512)
    normed_rope(B_Q, qn_ref, hd_scale)
    normed_rope(B_Q + LANES, qn_ref, hd_scale)
    normed_rope(B_K, kn_ref)
    plain(B_V, 128)
    plain(C_Q, 256, DIFF_QK_DIM ** -0.5 * LOG2E)
    plain(C_K, 512)
    plain(D_Q, 256, hd_scale)
    plain(D_K, 256)


def _in_proj(x2d, gain, w_qkv, cos_t, sin_t, qn, kn, seq):
    n_tok = x2d.shape[0]
    t = TOKEN_TILE
    tiles_per_seq = seq // t
    return pl.pallas_call(
        _in_proj_kernel,
        grid=(n_tok // t,),
        in_specs=[
            _row_spec(t, lambda i: i),
            _whole(), _whole(),
            pl.BlockSpec((t, LANES), lambda i: (i % tiles_per_seq, 0)),
            pl.BlockSpec((t, LANES), lambda i: (i % tiles_per_seq, 0)),
            _whole(), _whole(),
        ],
        out_specs=pl.BlockSpec((t, QKV_COLS), lambda i: (i, 0)),
        out_shape=jax.ShapeDtypeStruct((n_tok, QKV_COLS), BF16),
        compiler_params=_params(1),
        name="in_proj",
    )(x2d, gain, w_qkv, cos_t, sin_t, qn, kn)


def _na_kernel(q_ref, k_ref, v_ref, b_ref, o_ref, *, rows):
    t = pl.program_id(1)
    tq = NA_Q_ROWS * GRID_W
    tk = NA_K_ROWS * GRID_W
    r_start = jnp.clip(t * NA_Q_ROWS - NA_WIN_R // 2, 0, rows - NA_K_ROWS)
    k0 = pl.multiple_of(r_start * GRID_W, GRID_W)
    q = q_ref[0]
    kw = k_ref[0, pl.ds(k0, tk), :]
    vw = v_ref[0, pl.ds(k0, tk), :]
    lo_q, lo_k = _lo_lanes(tq), _lo_lanes(tk)
    for pair in range(2):
        sl = slice(pair * LANES, (pair + 1) * LANES)
        qp, kp, vp = q[:, sl], kw[:, sl], vw[:, sl]
        zero = jnp.zeros_like(qp)
        s_all = _nt_dot(jnp.concatenate([jnp.where(lo_q, qp, zero), jnp.where(lo_q, zero, qp)], axis=0), kp)
        halves = []
        for half in range(2):
            s = s_all[half * tq:(half + 1) * tq] + b_ref[0, 2 * pair + half]
            e = jnp.exp2(s - jnp.max(s, axis=-1, keepdims=True))
            halves.append(_normalise(_pv_with_rowsum(e, vp, _own(lo_k, half))))
        o_ref[0, :, sl] = jnp.where(lo_q, halves[0], halves[1]).astype(BF16)


def _na_bias_tables(rpb, rows):
    depth, heads = rpb.shape[0], rpb.shape[1]
    qc, kc = np.arange(GRID_W)[:, None], np.arange(GRID_W)[None, :]
    c0 = np.clip(qc - NA_WIN_C // 2, 0, GRID_W - NA_WIN_C)
    col_valid = (kc >= c0) & (kc < c0 + NA_WIN_C)
    dc = np.clip(kc - qc + NA_WIN_C - 1, 0, 2 * NA_WIN_C - 2)
    blocks = jnp.where(col_valid, rpb[:, :, :, dc] * LOG2E, NEG_INF).astype(F32)
    masked = jnp.full((depth, heads, GRID_W, GRID_W), NEG_INF, F32)
    cases = []
    for q_row0 in (0, NA_Q_ROWS, rows - NA_Q_ROWS):
        r_start = int(np.clip(q_row0 - NA_WIN_R // 2, 0, rows - NA_K_ROWS))
        strips = []
        for i in range(NA_Q_ROWS):
            qr = q_row0 + i
            r0 = int(np.clip(qr - NA_WIN_R // 2, 0, rows - NA_WIN_R))
            strip = []
            for j in range(NA_K_ROWS):
                kr = r_start + j
                strip.append(blocks[:, :, kr - qr + NA_WIN_R - 1] if r0 <= kr < r0 + NA_WIN_R else masked)
            strips.append(jnp.concatenate(strip, axis=-1))
        cases.append(jnp.concatenate(strips, axis=-2))
    return jnp.stack(cases, axis=1)


def _na_attention(qkv, bias, batch, seq):
    rows = seq // GRID_W
    n_t = rows // NA_Q_ROWS
    tq, tk = NA_Q_ROWS * GRID_W, NA_K_ROWS * GRID_W
    case = lambda t: jnp.where(t == 0, 0, jnp.where(t == n_t - 1, 2, 1))
    return pl.pallas_call(
        functools.partial(_na_kernel, rows=rows),
        grid=(batch, n_t),
        in_specs=[
            pl.BlockSpec((1, tq, 256), lambda b, t: (b, t, A_Q // 256)),
            pl.BlockSpec((1, seq, 256), lambda b, t: (b, 0, A_K // 256)),
            pl.BlockSpec((1, seq, 256), lambda b, t: (b, 0, A_V // 256)),
            pl.BlockSpec((1, 4, tq, tk), lambda b, t: (case(t), 0, 0, 0)),
        ],
        out_specs=pl.BlockSpec((1, tq, BRANCH_W), lambda b, t: (b, t, 0)),
        out_shape=jax.ShapeDtypeStruct((batch, seq, BRANCH_W), BF16),
        compiler_params=_params(2),
        name="na_attention",
    )(qkv, qkv, qkv, bias)


def _online_softmax_step(s, pv_fn, carry):
    m, acc = carry
    m_new = jnp.maximum(m, jnp.max(s, axis=-1, keepdims=True))
    alpha = jnp.exp2(m - m_new)
    return m_new, alpha * acc + pv_fn(jnp.exp2(s - m_new))


def _softmax_init(rows):
    return (jnp.full((rows, 1), NEG_INF, F32), jnp.zeros((rows, LANES), F32))


def _stack_group(q, sel):
    g0, g1 = q[:, :LANES], q[:, LANES:]
    zero = jnp.zeros_like(g0)
    return jnp.concatenate([jnp.where(sel, g0, zero), jnp.where(sel, g1, zero)], axis=0)


def _gqa_kernel(q_ref, k_ref, v_ref, o_ref, *, seq):
    tq, tk = GQA_TQ, GQA_TK
    q = q_ref[0]
    lo_q, lo_k = _lo_lanes(tq), _lo_lanes(tk)
    qq = [_stack_group(q, _own(lo_q, h)) for h in range(2)]

    def body(j, carry):
        k0 = pl.multiple_of(j * tk, tk)
        kj = k_ref[0, pl.ds(k0, tk), :]
        vj = v_ref[0, pl.ds(k0, tk), :]
        return tuple(
            _online_softmax_step(_nt_dot(qq[h], kj),
                                 functools.partial(_pv_with_rowsum, v=vj, own=_own(lo_k, h)), carry[h])
            for h in range(2))

    res = lax.fori_loop(0, seq // tk, body, (_softmax_init(2 * tq), _softmax_init(2 * tq)), unroll=2)
    n = [_normalise(acc) for _, acc in res]
    o_ref[0, :, :LANES] = jnp.where(lo_q, n[0][:tq], n[1][:tq]).astype(BF16)
    o_ref[0, :, LANES:] = jnp.where(lo_q, n[0][tq:], n[1][tq:]).astype(BF16)


def _gqa_attention(qkv, batch, seq):
    tq = GQA_TQ
    return pl.pallas_call(
        functools.partial(_gqa_kernel, seq=seq),
        grid=(batch, seq // tq),
        in_specs=[
            pl.BlockSpec((1, tq, 256), lambda b, t: (b, t, B_Q // 256)),
            pl.BlockSpec((1, seq, LANES), lambda b, t: (b, 0, B_K // LANES)),
            pl.BlockSpec((1, seq, LANES), lambda b, t: (b, 0, B_V // LANES)),
        ],
        out_specs=pl.BlockSpec((1, tq, BRANCH_W), lambda b, t: (b, t, 0)),
        out_shape=jax.ShapeDtypeStruct((batch, seq, BRANCH_W), BF16),
        compiler_params=_params(2),
        name="gqa_attention",
    )(qkv, qkv, qkv)


_AUG_KEY_LO, _AUG_KEY_HI, _AUG_ROW, _AUG_DELTA = 0, 3, 6, 9
_AUG_MASK = 12


def _bf16_pieces(x):
    x = np.asarray(x, np.float32)
    pieces = []
    for _ in range(3):
        p = x.astype(BF16).astype(np.float32)
        pieces.append(p)
        x = x - p
    return pieces


def _diff_bias_factors():
    tq, tk = DIFF_TQ, DIFF_TK
    q_side = np.zeros((4, tq, LANES), np.float32)
    for h, slope in enumerate(DIFF_SLOPES):
        c = np.float32(slope * LOG2E)
        c3 = _bf16_pieces(c)
        r3 = _bf16_pieces(-c * np.arange(tq, dtype=np.float32))
        for n in range(3):
            for g in (_AUG_KEY_LO, _AUG_KEY_HI, _AUG_DELTA):
                q_side[h, :, g + n] = c3[n]
            q_side[h, :, _AUG_ROW + n] = r3[n]
        q_side[h, :, _AUG_MASK] = 1.0
    k_side = np.zeros((tk, LANES), np.float32)
    j = np.arange(tk)
    for n in range(3):
        k_side[:, _AUG_KEY_LO + n] = j % 256
        k_side[:, _AUG_KEY_HI + n] = 256 * (j // 256)
        k_side[:, _AUG_ROW + n] = 1.0
    return jnp.asarray(q_side, BF16), jnp.asarray(k_side, F32)


def _diff_kernel(q_ref, k_ref, v_ref, qf_ref, kf_ref, lq1_ref, lk1_ref, lq2_ref, lk2_ref, subln_ref, o_ref, *,
                 seq, lambda_init):
    tq, tk = DIFF_TQ, DIFF_TK
    q_start = pl.multiple_of(pl.program_id(1) * tq, tq)
    q = q_ref[0]
    lam = (jnp.exp(jnp.sum(lq1_ref[...] * lk1_ref[...], axis=-1, keepdims=True))
           - jnp.exp(jnp.sum(lq2_ref[...] * lk2_ref[...], axis=-1, keepdims=True)) + lambda_init)
    lane = lax.broadcasted_iota(jnp.int32, (tq, LANES), 1)
    seg = lane // DIFF_QK_DIM
    lo_q, lo_k = lane < HEAD_DIM, _lo_lanes(tk)
    lane_k = lax.broadcasted_iota(jnp.int32, (tk, LANES), 1)
    key_k = lax.broadcasted_iota(jnp.int32, (tk, LANES), 0)
    delta_lanes = (lane_k >= _AUG_DELTA) & (lane_k < _AUG_DELTA + 3)
    mask_lane = lane_k == _AUG_MASK
    own_dist = jnp.abs(lax.broadcasted_iota(jnp.int32, (tq, tq), 0)
                       - lax.broadcasted_iota(jnp.int32, (tq, tq), 1)).astype(F32)

    def pv(p, v, lo):
        return jnp.concatenate([_pv_with_rowsum(p[:2 * tq], v, lo),
                                _pv_with_rowsum(p[2 * tq:], v, jnp.logical_not(lo))], axis=0)

    qq_aug, carry = [], []
    for pair in range(2):
        sl = slice(pair * LANES, (pair + 1) * LANES)
        qp = q[:, sl]
        zero = jnp.zeros_like(qp)
        rows4 = jnp.concatenate([jnp.where(seg == i, qp, zero) for i in range(4)], axis=0)
        factors = jnp.concatenate([qf_ref[2 * pair]] * 2 + [qf_ref[2 * pair + 1]] * 2, axis=0)
        qq_aug.append(jnp.concatenate([rows4, factors], axis=1))
        bias = jnp.concatenate([(DIFF_SLOPES[2 * pair] * LOG2E) * own_dist] * 2
                               + [(DIFF_SLOPES[2 * pair + 1] * LOG2E) * own_dist] * 2, axis=0)
        s = _nt_dot(rows4, k_ref[0, pl.ds(q_start, tq), sl]) - bias
        m = jnp.max(s, axis=-1, keepdims=True)
        carry.append((m, pv(jnp.exp2(s - m), v_ref[0, pl.ds(q_start, tq), sl], lo_q)))

    def body(j, carry):
        k0 = pl.multiple_of(j * tk, tk)
        rel = key_k + (k0 - q_start)
        sign = jnp.where(rel < 0, 1.0, -1.0)
        kf = sign * (kf_ref[...] - jnp.where(delta_lanes, (q_start - k0).astype(F32), 0.0))
        own_key = jnp.where(rel >= 0, jnp.where(rel < tq, NEG_INF, 0.0), 0.0)
        k_factors = jnp.where(mask_lane, own_key, kf).astype(BF16)
        new = []
        for pair in range(2):
            sl = slice(pair * LANES, (pair + 1) * LANES)
            s = _nt_dot(qq_aug[pair], jnp.concatenate([k_ref[0, pl.ds(k0, tk), sl], k_factors], axis=1))
            new.append(_online_softmax_step(
                s, functools.partial(pv, v=v_ref[0, pl.ds(k0, tk), sl], lo=lo_k), carry[pair]))
        return tuple(new)

    carry = lax.fori_loop(0, seq // tk, body, tuple(carry), unroll=2)
    for pair in range(2):
        p = _normalise(carry[pair][1])
        head_a = p[0:tq] - lam * p[tq:2 * tq]
        head_b = p[2 * tq:3 * tq] - lam * p[3 * tq:4 * tq]
        o = jnp.where(lo_q, head_a, head_b)
        o = o * _pair_rms_scale(o) * subln_ref[...] * (1.0 - lambda_init)
        o_ref[0, :, pair * LANES:(pair + 1) * LANES] = o.astype(BF16)


def _diff_attention(qkv, q_factors, k_factors, lq1, lk1, lq2, lk2, subln, batch, seq, lambda_init):
    tq = DIFF_TQ
    return pl.pallas_call(
        functools.partial(_diff_kernel, seq=seq, lambda_init=lambda_init),
        grid=(batch, seq // tq),
        in_specs=[
            pl.BlockSpec((1, tq, 256), lambda b, t: (b, t, C_Q // 256)),
            pl.BlockSpec((1, seq, 256), lambda b, t: (b, 0, C_K // 256)),
            pl.BlockSpec((1, seq, 256), lambda b, t: (b, 0, C_V // 256)),
        ] + [_whole()] * 7,
        out_specs=pl.BlockSpec((1, tq, BRANCH_W), lambda b, t: (b, t, 0)),
        out_shape=jax.ShapeDtypeStruct((batch, seq, BRANCH_W), BF16),
        compiler_params=_params(2),
        name="diff_attention",
    )(qkv, qkv, qkv, q_factors, k_factors, lq1, lk1, lq2, lk2, subln)


def _swa_kernel(q_ref, k_ref, v_ref, b_ref, sink_ref, o_ref, *, seq):
    tq = SWA_TQ
    span = tq + 2 * SWA_WINDOW
    q_start = pl.program_id(1) * tq
    k_start = pl.multiple_of(jnp.clip(q_start - SWA_WINDOW, 0, seq - span), SWA_WINDOW)
    q = q_ref[0]
    kw = k_ref[0, pl.ds(k_start, span), :]
    vw = v_ref[0, pl.ds(k_start, span), :]
    lo_q, lo_k = _lo_lanes(tq), _lo_lanes(span)
    outs = [[None, None], [None, None]]
    for kv_head in range(2):
        s_all = _nt_dot(_stack_group(q, _own(lo_q, kv_head)), kw)
        for g in range(2):
            head = 2 * kv_head + g
            sink = sink_ref[head] * LOG2E
            s = s_all[g * tq:(g + 1) * tq] + b_ref[0, head]
            m = jnp.maximum(jnp.max(s, axis=-1, keepdims=True), sink)
            r = _pv_with_rowsum(jnp.exp2(s - m), vw, _own(lo_k, kv_head))
            outs[g][kv_head] = r / (pltpu.roll(r, HEAD_DIM, 1) + jnp.exp2(sink - m))
    o_ref[0, :, :LANES] = jnp.where(lo_q, outs[0][0], outs[0][1]).astype(BF16)
    o_ref[0, :, LANES:] = jnp.where(lo_q, outs[1][0], outs[1][1]).astype(BF16)


def _swa_bias_tables():
    tq = SWA_TQ
    span = tq + 2 * SWA_WINDOW
    i, j = np.arange(tq)[:, None], np.arange(span)[None, :]
    cases = []
    for lead in (0, SWA_WINDOW, 2 * SWA_WINDOW):
        rel = np.abs(j - i - lead)
        cases.append(np.stack([np.where(rel <= SWA_WINDOW, -(slope * LOG2E) * rel, NEG_INF) for slope in SWA_SLOPES]))
    return jnp.asarray(np.stack(cases), F32)


def _swa_attention(qkv, sink, batch, seq):
    tq = SWA_TQ
    span = tq + 2 * SWA_WINDOW
    n_t = seq // tq
    case = lambda t: jnp.where(t == 0, 0, jnp.where(t == n_t - 1, 2, 1))
    return pl.pallas_call(
        functools.partial(_swa_kernel, seq=seq),
        grid=(batch, n_t),
        in_specs=[
            pl.BlockSpec((1, tq, 256), lambda b, t: (b, t, D_Q // 256)),
            pl.BlockSpec((1, seq, LANES), lambda b, t: (b, 0, D_K // LANES)),
            pl.BlockSpec((1, seq, LANES), lambda b, t: (b, 0, D_V // LANES)),
            pl.BlockSpec((1, 4, tq, span), lambda b, t: (case(t), 0, 0, 0)),
            _whole(pltpu.SMEM),
        ],
        out_specs=pl.BlockSpec((1, tq, BRANCH_W), lambda b, t: (b, t, 0)),
        out_shape=jax.ShapeDtypeStruct((batch, seq, BRANCH_W), BF16),
        compiler_params=_params(2),
        name="swa_attention",
    )(qkv, qkv, qkv, _swa_bias_tables(), sink)


def _merge_kernel(x_ref, oa_ref, ob_ref, oc_ref, od_ref, gpre_ref, wg_ref, wb_ref, wo_ref, gpost_ref, o_ref):
    x = x_ref[...]
    h = _rmsnorm(x, gpre_ref[...]).astype(BF16)
    merged = None
    for i, br_ref in enumerate((oa_ref, ob_ref, oc_ref, od_ref)):
        gate = jax.nn.sigmoid(jnp.dot(h, wg_ref[:, i * D_MODEL:(i + 1) * D_MODEL], preferred_element_type=F32))
        term = gate * jnp.dot(br_ref[...], wb_ref[i], preferred_element_type=F32)
        merged = term if merged is None else merged + term
    half = x.shape[0] // 2
    for r0 in (0, half):
        mix = jnp.dot(merged[r0:r0 + half].astype(BF16), wo_ref[...], preferred_element_type=F32)
        o_ref[r0:r0 + half, :] = x[r0:r0 + half] + _rmsnorm(mix, gpost_ref[...])


def _merge(x2d, branches, gpre, w_gate, w_branch, w_out, gpost):
    n_tok = x2d.shape[0]
    t = TOKEN_TILE
    tok = lambda w: pl.BlockSpec((t, w), lambda i: (i, 0))
    return pl.pallas_call(
        _merge_kernel,
        grid=(n_tok // t,),
        in_specs=[_row_spec(t, lambda i: i)] + [tok(BRANCH_W)] * 4 + [_whole()] * 5,
        out_specs=_row_spec(t, lambda i: i),
        out_shape=jax.ShapeDtypeStruct((n_tok, D_MODEL), F32),
        compiler_params=_params(1),
        name="merge",
    )(x2d, *branches, gpre, w_gate, w_branch, w_out, gpost)


def _gelu_tanh(x):
    k = -2.0 * LOG2E * math.sqrt(2.0 / math.pi)
    return x / (1.0 + jnp.exp2(x * (k + (k * 0.044715) * (x * x))))


def _ffn_kernel(xp_ref, x_ref, xn_ref, gpre_ref, wup_ref, cw_ref, cb_ref, wdn_ref, gpost_ref, o_ref,
                h_scr, a_scr, *, tiles_per_seq):
    t = TOKEN_TILE
    n_b = t // 8
    pos = pl.program_id(0) % tiles_per_seq
    gpre = gpre_ref[...]
    x = jnp.swapaxes(x_ref[...].reshape(8, n_b, D_MODEL), 0, 1).reshape(t, D_MODEL)
    h_scr[0:t, :] = _rmsnorm(x, gpre).astype(BF16)
    keep_prev = (pos != 0).astype(F32)
    keep_next = (pos != tiles_per_seq - 1).astype(F32)
    hp = pltpu.roll(_rmsnorm(xp_ref[...], gpre) * keep_prev, 1, 0)
    hn = pltpu.roll(_rmsnorm(xn_ref[...], gpre) * keep_next, 1, 0)
    hrow = lax.broadcasted_iota(jnp.int32, (HALO, D_MODEL), 0)
    h_scr[t:, :] = jnp.where(hrow == 0, hp, jnp.where(hrow == 1, hn, 0.0)).astype(BF16)
    h = h_scr[...]
    sub = lax.broadcasted_iota(jnp.int32, (8, FFN_CHUNK), 0)

    def conv(c0):
        u = jnp.dot(h, wup_ref[:, c0:c0 + FFN_CHUNK], preferred_element_type=F32)
        halo = u[t:t + 8]
        first_prev = jnp.where(sub == 0, halo[0:1], pltpu.roll(u[t - 8:t], 1, 0))
        last_next = jnp.where(sub == 7, halo[1:2], pltpu.roll(u[0:8], 7, 0))
        prev = jnp.concatenate([first_prev, u[0:t - 8]], axis=0)
        nxt = jnp.concatenate([u[8:t], last_next], axis=0)
        w = cw_ref[:, c0:c0 + FFN_CHUNK]
        return cb_ref[:, c0:c0 + FFN_CHUNK] + prev * w[0:1] + u[0:t] * w[1:2] + nxt * w[2:3]

    for ch in range(D_FF // FFN_CHUNK):
        gate = conv(ch * FFN_CHUNK)
        val = conv(D_FF + ch * FFN_CHUNK)
        a_scr[:, ch * FFN_CHUNK:(ch + 1) * FFN_CHUNK] = (_gelu_tanh(gate) * val).astype(BF16)
    half = t // 2
    out = jnp.concatenate(
        [x[r0:r0 + half] + _rmsnorm(jnp.dot(a_scr[r0:r0 + half, :], wdn_ref[...], preferred_element_type=F32),
                                    gpost_ref[...]) for r0 in (0, half)], axis=0)
    out = jnp.swapaxes(out.reshape(n_b, 8, D_MODEL), 0, 1).reshape(t, D_MODEL)
    o_ref[...] = out


def _ffn(x2d, gpre, w_up, conv_w, conv_b, w_down, gpost, seq):
    n_tok = x2d.shape[0]
    t = TOKEN_TILE
    per_tile = t // HALO
    n_halo_blocks = n_tok // HALO
    return pl.pallas_call(
        functools.partial(_ffn_kernel, tiles_per_seq=seq // t),
        grid=(n_tok // t,),
        in_specs=[
            _row_spec(HALO, lambda i: jnp.maximum(i * per_tile - 1, 0)),
            _row_spec(t, lambda i: i),
            _row_spec(HALO, lambda i: jnp.minimum((i + 1) * per_tile, n_halo_blocks - 1)),
        ] + [_whole()] * 6,
        out_specs=_row_spec(t, lambda i: i),
        out_shape=jax.ShapeDtypeStruct((n_tok, D_MODEL), F32),
        scratch_shapes=[
            pltpu.VMEM((t + HALO, D_MODEL), BF16),
            pltpu.VMEM((t, D_FF), BF16),
        ],
        compiler_params=_params(1),
        name="conv_ffn",
    )(x2d, x2d, x2d, gpre, w_up, conv_w, conv_b, w_down, gpost)


def _rope_tables(seq):
    t = np.arange(seq)
    axis_dim = HEAD_DIM // 2
    inv = jnp.asarray(ROPE_THETA, F32) ** (-jnp.arange(0, axis_dim, 2, dtype=F32) / axis_dim)
    ang_r = jnp.asarray(t // GRID_W, F32)[:, None] * inv
    ang_c = jnp.asarray(t % GRID_W, F32)[:, None] * inv
    cos = jnp.concatenate([jnp.cos(ang_r)] * 2 + [jnp.cos(ang_c)] * 2, axis=-1)
    sin = jnp.concatenate([-jnp.sin(ang_r), jnp.sin(ang_r), -jnp.sin(ang_c), jnp.sin(ang_c)], axis=-1)
    return jnp.tile(cos, (1, 2)), jnp.tile(sin, (1, 2))


def _group_major(w, axis):
    shape = w.shape
    w = w.reshape(shape[:axis] + (2, 2, HEAD_DIM) + shape[axis + 1:])
    return jnp.swapaxes(w, axis, axis + 1).reshape(shape)


def kernel(x_prompt, x_sample, norm_mix_pre, norm_mix_post, norm_ffn_pre, norm_ffn_post, w_in, na_rpb, gqa_q_norm, gqa_k_norm, diff_lambda_q1, diff_lambda_k1, diff_lambda_q2, diff_lambda_k2, diff_subln, swa_sink, w_branch, w_out, ffn_w_up, ffn_conv_w, ffn_conv_b, ffn_w_down):
    depth = w_in.shape[0]
    w_qkv = jnp.concatenate([
        w_in[:, :, :B_Q], _group_major(w_in[:, :, B_Q:B_K], 2), w_in[:, :, B_K:D_Q],
        _group_major(w_in[:, :, D_Q:D_K], 2), w_in[:, :, D_K:QKV_COLS]], axis=-1).astype(BF16)
    w_gate = w_in[:, :, QKV_COLS:].astype(BF16)
    w_br = w_branch.reshape(depth, N_BRANCH, BRANCH_W, D_MODEL)
    w_br = jnp.stack([w_br[:, 0], _group_major(w_br[:, 1], 1), w_br[:, 2], _group_major(w_br[:, 3], 1)],
                     axis=1).astype(BF16)
    diff_qf, diff_kf = _diff_bias_factors()
    w_o = w_out.astype(BF16)
    w_up = ffn_w_up.astype(BF16)
    w_dn = ffn_w_down.astype(BF16)
    row2 = lambda a: a.astype(F32).reshape(depth, 1, -1)
    tile2 = lambda a: jnp.tile(a.astype(F32), (1, 2)).reshape(depth, 1, -1)
    g_mix_pre, g_mix_post = row2(norm_mix_pre), row2(norm_mix_post)
    g_ffn_pre, g_ffn_post = row2(norm_ffn_pre), row2(norm_ffn_post)
    qn, kn, subln = tile2(gqa_q_norm), tile2(gqa_k_norm), tile2(diff_subln)
    lq1, lk1, lq2, lk2 = (row2(a) for a in (diff_lambda_q1, diff_lambda_k1, diff_lambda_q2, diff_lambda_k2))
    conv_b = row2(ffn_conv_b)
    conv_w = ffn_conv_w.astype(F32)
    sink = swa_sink.astype(F32)

    def run_trunk(x):
        batch, seq, _ = x.shape
        cos_t, sin_t = _rope_tables(seq)
        na_bias = _na_bias_tables(na_rpb.astype(F32), seq // GRID_W)
        x2d = x.reshape(batch * seq, D_MODEL)
        for l in range(depth):
            lambda_init = 0.8 - 0.6 * math.exp(-0.3 * l)
            qkv = _in_proj(x2d, g_mix_pre[l], w_qkv[l], cos_t, sin_t, qn[l], kn[l], seq)
            qkv = qkv.reshape(batch, seq, QKV_COLS)
            o_a = _na_attention(qkv, na_bias[l], batch, seq)
            o_b = _gqa_attention(qkv, batch, seq)
            o_c = _diff_attention(qkv, diff_qf, diff_kf, lq1[l], lk1[l], lq2[l], lk2[l], subln[l], batch, seq,
                                  lambda_init)
            o_d = _swa_attention(qkv, sink[l], batch, seq)
            branches = [o.reshape(batch * seq, BRANCH_W) for o in (o_a, o_b, o_c, o_d)]
            x2d = _merge(x2d, branches, g_mix_pre[l], w_gate[l], w_br[l], w_o[l], g_mix_post[l])
            x2d = _ffn(x2d, g_ffn_pre[l], w_up[l], conv_w[l], conv_b[l], w_dn[l], g_ffn_post[l], seq)
        return x2d.reshape(batch, seq, D_MODEL)

    return (run_trunk(x_prompt), run_trunk(x_sample))
```

```python
import functools
import math

import numpy as np
import jax
import jax.numpy as jnp
from jax import lax
from jax.experimental import pallas as pl
from jax.experimental.pallas import tpu as pltpu

F32 = jnp.float32
BF16 = jnp.bfloat16

D_MODEL = 1024
GRID_W = 64
HEAD_DIM = 64
EPS = 1e-6
NEG_INF = -1e30
LOG2E = math.log2(math.e)
NA_WIN_R = 8
NA_WIN_C = 16
ROPE_THETA = 10000.0
DIFF_QK_DIM = 32
SWA_WINDOW = 128
N_ALIBI_HEADS = 8
N_BRANCH = 4
BRANCH_W = 256
D_FF = 2816
QKV_COLS = 2560

A_Q, A_K, A_V = 0, 256, 512
B_Q, B_K, B_V = 768, 1024, 1152
C_Q, C_K, C_V = 1280, 1536, 1792
D_Q, D_K, D_V = 2048, 2304, 2432

LANES = 128
VMEM_LIMIT = 56 * 1024 * 1024

TOKEN_TILE = 1024
HALO = 16
FFN_CHUNK = 256
NA_Q_ROWS = 4
NA_K_ROWS = 12
GQA_TQ, GQA_TK = 512, 2048
DIFF_TQ, DIFF_TK = 256, 2048
SWA_TQ = 256


def _alibi_slopes():
    s = 2.0 ** (-8.0 * np.arange(1, N_ALIBI_HEADS + 1) / N_ALIBI_HEADS)
    return [float(v) for v in s[0::2]], [float(v) for v in s[1::2]]


DIFF_SLOPES, SWA_SLOPES = _alibi_slopes()


def _params(n_parallel):
    return pltpu.CompilerParams(dimension_semantics=("parallel",) * n_parallel,
                                vmem_limit_bytes=VMEM_LIMIT)


def _whole(space=pltpu.VMEM):
    return pl.BlockSpec(memory_space=space)


def _row_spec(rows, index):
    return pl.BlockSpec((rows, D_MODEL), lambda i: (index(i), 0))


def _rmsnorm(x, gain):
    ms = jnp.mean(x * x, axis=-1, keepdims=True)
    return x * lax.rsqrt(ms + EPS) * gain


def _pair_rms_scale(y):
    lo = lax.broadcasted_iota(jnp.int32, y.shape, 1) < HEAD_DIM
    y2 = y * y
    s_lo = jnp.sum(jnp.where(lo, y2, 0.0), axis=-1, keepdims=True)
    s_hi = jnp.sum(jnp.where(lo, 0.0, y2), axis=-1, keepdims=True)
    ms = jnp.where(lo, s_lo, s_hi) * (1.0 / HEAD_DIM)
    return lax.rsqrt(ms + EPS)


def _edge_case(t, n_t):
    return jnp.where(t == 0, 0, jnp.where(t == n_t - 1, 2, 1))


def _nt_dot(a, b):
    return lax.dot_general(a, b, (((1,), (1,)), ((), ())), preferred_element_type=F32)


def _lo_lanes(rows):
    return lax.broadcasted_iota(jnp.int32, (rows, LANES), 1) < HEAD_DIM


def _own(lo, half):
    return lo if half == 0 else jnp.logical_not(lo)


def _pv_with_rowsum(p, v, own):
    return jnp.dot(p.astype(BF16), jnp.where(own, v, jnp.ones_like(v)), preferred_element_type=F32)


def _normalise(acc):
    return acc / pltpu.roll(acc, HEAD_DIM, 1)


def _in_proj_kernel(x_ref, gain_ref, w_ref, cos_ref, sin_ref, qn_ref, kn_ref, o_ref):
    h = _rmsnorm(x_ref[...], gain_ref[...]).astype(BF16)

    def proj(c0, width):
        return jnp.dot(h, w_ref[:, c0:c0 + width], preferred_element_type=F32)

    def plain(c0, width, scale=None):
        y = proj(c0, width)
        if scale is not None:
            y = y * scale
        o_ref[:, c0:c0 + width] = y.astype(BF16)

    def normed_rope(c0, gain_ref_, scale=None):
        y = proj(c0, LANES)
        yn = y * _pair_rms_scale(y) * gain_ref_[...]
        lane = lax.broadcasted_iota(jnp.int32, yn.shape, 1)
        partner = jnp.where((lane % 32) < 16, pltpu.roll(yn, LANES - 16, 1), pltpu.roll(yn, 16, 1))
        out = yn * cos_ref[...] + partner * sin_ref[...]
        if scale is not None:
            out = out * scale
        o_ref[:, c0:c0 + LANES] = out.astype(BF16)

    hd_scale = HEAD_DIM ** -0.5 * LOG2E
    plain(A_Q, 256, hd_scale)
    plain(A_K, 512)
    normed_rope(B_Q, qn_ref, hd_scale)
    normed_rope(B_Q + LANES, qn_ref, hd_scale)
    normed_rope(B_K, kn_ref)
    plain(B_V, 128)
    plain(C_Q, 256, DIFF_QK_DIM ** -0.5 * LOG2E)
    plain(C_K, 512)
    plain(D_Q, 256, hd_scale)
    plain(D_K, 256)


def _in_proj(x2d, gain, w_qkv, cos_t, sin_t, qn, kn, seq):
    n_tok = x2d.shape[0]
    t = TOKEN_TILE
    tiles_per_seq = seq // t
    return pl.pallas_call(
        _in_proj_kernel,
        grid=(n_tok // t,),
        in_specs=[
            _row_spec(t, lambda i: i),
            _whole(), _whole(),
            pl.BlockSpec((t, LANES), lambda i: (i % tiles_per_seq, 0)),
            pl.BlockSpec((t, LANES), lambda i: (i % tiles_per_seq, 0)),
            _whole(), _whole(),
        ],
        out_specs=pl.BlockSpec((t, QKV_COLS), lambda i: (i, 0)),
        out_shape=jax.ShapeDtypeStruct((n_tok, QKV_COLS), BF16),
        compiler_params=_params(1),
        name="in_proj",
    )(x2d, gain, w_qkv, cos_t, sin_t, qn, kn)


def _na_kernel(q_ref, k_ref, v_ref, b_ref, o_ref, *, rows):
    t = pl.program_id(1)
    case = _edge_case(t, rows // NA_Q_ROWS)
    tq = NA_Q_ROWS * GRID_W
    tk = NA_K_ROWS * GRID_W
    r_start = jnp.clip(t * NA_Q_ROWS - NA_WIN_R // 2, 0, rows - NA_K_ROWS)
    k0 = pl.multiple_of(r_start * GRID_W, GRID_W)
    q = q_ref[0]
    kw = k_ref[0, pl.ds(k0, tk), :]
    vw = v_ref[0, pl.ds(k0, tk), :]
    lo_q, lo_k = _lo_lanes(tq), _lo_lanes(tk)
    for pair in range(2):
        sl = slice(pair * LANES, (pair + 1) * LANES)
        qp, kp, vp = q[:, sl], kw[:, sl], vw[:, sl]
        zero = jnp.zeros_like(qp)
        s_all = _nt_dot(jnp.concatenate([jnp.where(lo_q, qp, zero), jnp.where(lo_q, zero, qp)], axis=0), kp)
        halves = []
        for half in range(2):
            s = s_all[half * tq:(half + 1) * tq] + b_ref[case, 2 * pair + half]
            e = jnp.exp2(s - jnp.max(s, axis=-1, keepdims=True))
            halves.append(_normalise(_pv_with_rowsum(e, vp, _own(lo_k, half))))
        o_ref[0, :, sl] = jnp.where(lo_q, halves[0], halves[1]).astype(BF16)


def _na_bias_tables(rpb, rows):
    depth, heads = rpb.shape[0], rpb.shape[1]
    qc, kc = np.arange(GRID_W)[:, None], np.arange(GRID_W)[None, :]
    c0 = np.clip(qc - NA_WIN_C // 2, 0, GRID_W - NA_WIN_C)
    col_valid = (kc >= c0) & (kc < c0 + NA_WIN_C)
    dc = np.clip(kc - qc + NA_WIN_C - 1, 0, 2 * NA_WIN_C - 2)
    blocks = jnp.where(col_valid, rpb[:, :, :, dc] * LOG2E, NEG_INF).astype(F32)
    masked = jnp.full((depth, heads, GRID_W, GRID_W), NEG_INF, F32)
    cases = []
    for q_row0 in (0, NA_Q_ROWS, rows - NA_Q_ROWS):
        r_start = int(np.clip(q_row0 - NA_WIN_R // 2, 0, rows - NA_K_ROWS))
        strips = []
        for i in range(NA_Q_ROWS):
            qr = q_row0 + i
            r0 = int(np.clip(qr - NA_WIN_R // 2, 0, rows - NA_WIN_R))
            strip = []
            for j in range(NA_K_ROWS):
                kr = r_start + j
                strip.append(blocks[:, :, kr - qr + NA_WIN_R - 1] if r0 <= kr < r0 + NA_WIN_R else masked)
            strips.append(jnp.concatenate(strip, axis=-1))
        cases.append(jnp.concatenate(strips, axis=-2))
    return jnp.stack(cases, axis=1)


def _na_attention(qkv, bias, batch, seq):
    rows = seq // GRID_W
    n_t = rows // NA_Q_ROWS
    tq, tk = NA_Q_ROWS * GRID_W, NA_K_ROWS * GRID_W
    return pl.pallas_call(
        functools.partial(_na_kernel, rows=rows),
        grid=(batch, n_t),
        in_specs=[
            pl.BlockSpec((1, tq, 256), lambda b, t: (b, t, A_Q // 256)),
            pl.BlockSpec((1, seq, 256), lambda b, t: (b, 0, A_K // 256)),
            pl.BlockSpec((1, seq, 256), lambda b, t: (b, 0, A_V // 256)),
            _whole(),
        ],
        out_specs=pl.BlockSpec((1, tq, BRANCH_W), lambda b, t: (b, t, 0)),
        out_shape=jax.ShapeDtypeStruct((batch, seq, BRANCH_W), BF16),
        compiler_params=_params(2),
        name="na_attention",
    )(qkv, qkv, qkv, bias)


def _online_softmax_step(s, pv_fn, carry):
    m, acc = carry
    m_new = jnp.maximum(m, jnp.max(s, axis=-1, keepdims=True))
    alpha = jnp.exp2(m - m_new)
    return m_new, alpha * acc + pv_fn(jnp.exp2(s - m_new))


def _softmax_init(rows):
    return (jnp.full((rows, 1), NEG_INF, F32), jnp.zeros((rows, LANES), F32))


def _stack_group(q, sel):
    g0, g1 = q[:, :LANES], q[:, LANES:]
    zero = jnp.zeros_like(g0)
    return jnp.concatenate([jnp.where(sel, g0, zero), jnp.where(sel, g1, zero)], axis=0)


def _gqa_kernel(q_ref, k_ref, v_ref, o_ref, *, seq):
    tq, tk = GQA_TQ, GQA_TK
    q = q_ref[0]
    lo_q, lo_k = _lo_lanes(tq), _lo_lanes(tk)
    qq = [_stack_group(q, _own(lo_q, h)) for h in range(2)]

    def body(j, carry):
        k0 = pl.multiple_of(j * tk, tk)
        kj = k_ref[0, pl.ds(k0, tk), :]
        vj = v_ref[0, pl.ds(k0, tk), :]
        return tuple(
            _online_softmax_step(_nt_dot(qq[h], kj),
                                 functools.partial(_pv_with_rowsum, v=vj, own=_own(lo_k, h)), carry[h])
            for h in range(2))

    res = lax.fori_loop(0, seq // tk, body, (_softmax_init(2 * tq), _softmax_init(2 * tq)), unroll=2)
    n = [_normalise(acc) for _, acc in res]
    o_ref[0, :, :LANES] = jnp.where(lo_q, n[0][:tq], n[1][:tq]).astype(BF16)
    o_ref[0, :, LANES:] = jnp.where(lo_q, n[0][tq:], n[1][tq:]).astype(BF16)


def _gqa_attention(qkv, batch, seq):
    tq = GQA_TQ
    return pl.pallas_call(
        functools.partial(_gqa_kernel, seq=seq),
        grid=(batch, seq // tq),
        in_specs=[
            pl.BlockSpec((1, tq, 256), lambda b, t: (b, t, B_Q // 256)),
            pl.BlockSpec((1, seq, LANES), lambda b, t: (b, 0, B_K // LANES)),
            pl.BlockSpec((1, seq, LANES), lambda b, t: (b, 0, B_V // LANES)),
        ],
        out_specs=pl.BlockSpec((1, tq, BRANCH_W), lambda b, t: (b, t, 0)),
        out_shape=jax.ShapeDtypeStruct((batch, seq, BRANCH_W), BF16),
        compiler_params=_params(2),
        name="gqa_attention",
    )(qkv, qkv, qkv)


_AUG_KEY_LO, _AUG_KEY_HI, _AUG_ROW, _AUG_DELTA = 0, 3, 6, 9
_AUG_MASK = 12


def _bf16_pieces(x):
    x = np.asarray(x, np.float32)
    pieces = []
    for _ in range(3):
        p = x.astype(BF16).astype(np.float32)
        pieces.append(p)
        x = x - p
    return pieces


def _diff_bias_factors():
    tq, tk = DIFF_TQ, DIFF_TK
    q_side = np.zeros((4, tq, LANES), np.float32)
    for h, slope in enumerate(DIFF_SLOPES):
        c = np.float32(slope * LOG2E)
        c3 = _bf16_pieces(c)
        r3 = _bf16_pieces(-c * np.arange(tq, dtype=np.float32))
        for n in range(3):
            for g in (_AUG_KEY_LO, _AUG_KEY_HI, _AUG_DELTA):
                q_side[h, :, g + n] = c3[n]
            q_side[h, :, _AUG_ROW + n] = r3[n]
        q_side[h, :, _AUG_MASK] = 1.0
    k_side = np.zeros((tk, LANES), np.float32)
    j = np.arange(tk)
    for n in range(3):
        k_side[:, _AUG_KEY_LO + n] = j % 256
        k_side[:, _AUG_KEY_HI + n] = 256 * (j // 256)
        k_side[:, _AUG_ROW + n] = 1.0
    return jnp.asarray(q_side, BF16), jnp.asarray(k_side, F32)


def _diff_kernel(q_ref, k_ref, v_ref, qf_ref, kf_ref, lq1_ref, lk1_ref, lq2_ref, lk2_ref, subln_ref, o_ref, *,
                 seq, lambda_init):
    tq, tk = DIFF_TQ, DIFF_TK
    q_start = pl.multiple_of(pl.program_id(1) * tq, tq)
    q = q_ref[0]
    lam = (jnp.exp(jnp.sum(lq1_ref[...] * lk1_ref[...], axis=-1, keepdims=True))
           - jnp.exp(jnp.sum(lq2_ref[...] * lk2_ref[...], axis=-1, keepdims=True)) + lambda_init)
    lane = lax.broadcasted_iota(jnp.int32, (tq, LANES), 1)
    seg = lane // DIFF_QK_DIM
    lo_q, lo_k = lane < HEAD_DIM, _lo_lanes(tk)
    lane_k = lax.broadcasted_iota(jnp.int32, (tk, LANES), 1)
    key_k = lax.broadcasted_iota(jnp.int32, (tk, LANES), 0)
    delta_lanes = (lane_k >= _AUG_DELTA) & (lane_k < _AUG_DELTA + 3)
    mask_lane = lane_k == _AUG_MASK
    own_dist = jnp.abs(lax.broadcasted_iota(jnp.int32, (tq, tq), 0)
                       - lax.broadcasted_iota(jnp.int32, (tq, tq), 1)).astype(F32)

    def pv(p, v, lo):
        return jnp.concatenate([_pv_with_rowsum(p[:2 * tq], v, lo),
                                _pv_with_rowsum(p[2 * tq:], v, jnp.logical_not(lo))], axis=0)

    qq_aug, carry = [], []
    for pair in range(2):
        sl = slice(pair * LANES, (pair + 1) * LANES)
        qp = q[:, sl]
        zero = jnp.zeros_like(qp)
        rows4 = jnp.concatenate([jnp.where(seg == i, qp, zero) for i in range(4)], axis=0)
        factors = jnp.concatenate([qf_ref[2 * pair]] * 2 + [qf_ref[2 * pair + 1]] * 2, axis=0)
        qq_aug.append(jnp.concatenate([rows4, factors], axis=1))
        bias = jnp.concatenate([(DIFF_SLOPES[2 * pair] * LOG2E) * own_dist] * 2
                               + [(DIFF_SLOPES[2 * pair + 1] * LOG2E) * own_dist] * 2, axis=0)
        s = _nt_dot(rows4, k_ref[0, pl.ds(q_start, tq), sl]) - bias
        m = jnp.max(s, axis=-1, keepdims=True)
        carry.append((m, pv(jnp.exp2(s - m), v_ref[0, pl.ds(q_start, tq), sl], lo_q)))

    def body(j, carry):
        k0 = pl.multiple_of(j * tk, tk)
        rel = key_k + (k0 - q_start)
        sign = jnp.where(rel < 0, 1.0, -1.0)
        kf = sign * (kf_ref[...] - jnp.where(delta_lanes, (q_start - k0).astype(F32), 0.0))
        own_key = jnp.where(rel >= 0, jnp.where(rel < tq, NEG_INF, 0.0), 0.0)
        k_factors = jnp.where(mask_lane, own_key, kf).astype(BF16)
        new = []
        for pair in range(2):
            sl = slice(pair * LANES, (pair + 1) * LANES)
            s = _nt_dot(qq_aug[pair], jnp.concatenate([k_ref[0, pl.ds(k0, tk), sl], k_factors], axis=1))
            new.append(_online_softmax_step(
                s, functools.partial(pv, v=v_ref[0, pl.ds(k0, tk), sl], lo=lo_k), carry[pair]))
        return tuple(new)

    carry = lax.fori_loop(0, seq // tk, body, tuple(carry), unroll=2)
    for pair in range(2):
        p = _normalise(carry[pair][1])
        head_a = p[0:tq] - lam * p[tq:2 * tq]
        head_b = p[2 * tq:3 * tq] - lam * p[3 * tq:4 * tq]
        o = jnp.where(lo_q, head_a, head_b)
        o = o * _pair_rms_scale(o) * subln_ref[...] * (1.0 - lambda_init)
        o_ref[0, :, pair * LANES:(pair + 1) * LANES] = o.astype(BF16)


def _diff_attention(qkv, q_factors, k_factors, lq1, lk1, lq2, lk2, subln, batch, seq, lambda_init):
    tq = DIFF_TQ
    return pl.pallas_call(
        functools.partial(_diff_kernel, seq=seq, lambda_init=lambda_init),
        grid=(batch, seq // tq),
        in_specs=[
            pl.BlockSpec((1, tq, 256), lambda b, t: (b, t, C_Q // 256)),
            pl.BlockSpec((1, seq, 256), lambda b, t: (b, 0, C_K // 256)),
            pl.BlockSpec((1, seq, 256), lambda b, t: (b, 0, C_V // 256)),
        ] + [_whole()] * 7,
        out_specs=pl.BlockSpec((1, tq, BRANCH_W), lambda b, t: (b, t, 0)),
        out_shape=jax.ShapeDtypeStruct((batch, seq, BRANCH_W), BF16),
        compiler_params=_params(2),
        name="diff_attention",
    )(qkv, qkv, qkv, q_factors, k_factors, lq1, lk1, lq2, lk2, subln)


def _swa_kernel(q_ref, k_ref, v_ref, b_ref, sink_ref, o_ref, *, seq):
    tq = SWA_TQ
    span = tq + 2 * SWA_WINDOW
    q_start = pl.program_id(1) * tq
    k_start = pl.multiple_of(jnp.clip(q_start - SWA_WINDOW, 0, seq - span), SWA_WINDOW)
    case = _edge_case(pl.program_id(1), seq // tq)
    q = q_ref[0]
    kw = k_ref[0, pl.ds(k_start, span), :]
    vw = v_ref[0, pl.ds(k_start, span), :]
    lo_q, lo_k = _lo_lanes(tq), _lo_lanes(span)
    outs = [[None, None], [None, None]]
    for kv_head in range(2):
        s_all = _nt_dot(_stack_group(q, _own(lo_q, kv_head)), kw)
        for g in range(2):
            head = 2 * kv_head + g
            sink = sink_ref[head] * LOG2E
            s = s_all[g * tq:(g + 1) * tq] + b_ref[case, head]
            m = jnp.maximum(jnp.max(s, axis=-1, keepdims=True), sink)
            r = _pv_with_rowsum(jnp.exp2(s - m), vw, _own(lo_k, kv_head))
            outs[g][kv_head] = r / (pltpu.roll(r, HEAD_DIM, 1) + jnp.exp2(sink - m))
    o_ref[0, :, :LANES] = jnp.where(lo_q, outs[0][0], outs[0][1]).astype(BF16)
    o_ref[0, :, LANES:] = jnp.where(lo_q, outs[1][0], outs[1][1]).astype(BF16)


def _swa_bias_tables():
    tq = SWA_TQ
    span = tq + 2 * SWA_WINDOW
    i, j = np.arange(tq)[:, None], np.arange(span)[None, :]
    cases = []
    for lead in (0, SWA_WINDOW, 2 * SWA_WINDOW):
        rel = np.abs(j - i - lead)
        cases.append(np.stack([np.where(rel <= SWA_WINDOW, -(slope * LOG2E) * rel, NEG_INF) for slope in SWA_SLOPES]))
    return jnp.asarray(np.stack(cases), F32)


def _swa_attention(qkv, sink, batch, seq):
    tq = SWA_TQ
    span = tq + 2 * SWA_WINDOW
    n_t = seq // tq
    return pl.pallas_call(
        functools.partial(_swa_kernel, seq=seq),
        grid=(batch, n_t),
        in_specs=[
            pl.BlockSpec((1, tq, 256), lambda b, t: (b, t, D_Q // 256)),
            pl.BlockSpec((1, seq, LANES), lambda b, t: (b, 0, D_K // LANES)),
            pl.BlockSpec((1, seq, LANES), lambda b, t: (b, 0, D_V // LANES)),
            _whole(),
            _whole(pltpu.SMEM),
        ],
        out_specs=pl.BlockSpec((1, tq, BRANCH_W), lambda b, t: (b, t, 0)),
        out_shape=jax.ShapeDtypeStruct((batch, seq, BRANCH_W), BF16),
        compiler_params=_params(2),
        name="swa_attention",
    )(qkv, qkv, qkv, _swa_bias_tables(), sink)


def _merge_kernel(x_ref, oa_ref, ob_ref, oc_ref, od_ref, gpre_ref, wg_ref, wb_ref, wo_ref, gpost_ref, o_ref):
    x = x_ref[...]
    h = _rmsnorm(x, gpre_ref[...]).astype(BF16)
    merged = None
    for i, br_ref in enumerate((oa_ref, ob_ref, oc_ref, od_ref)):
        gate = jax.nn.sigmoid(jnp.dot(h, wg_ref[:, i * D_MODEL:(i + 1) * D_MODEL], preferred_element_type=F32))
        term = gate * jnp.dot(br_ref[...], wb_ref[i], preferred_element_type=F32)
        merged = term if merged is None else merged + term
    half = x.shape[0] // 2
    for r0 in (0, half):
        mix = jnp.dot(merged[r0:r0 + half].astype(BF16), wo_ref[...], preferred_element_type=F32)
        o_ref[r0:r0 + half, :] = x[r0:r0 + half] + _rmsnorm(mix, gpost_ref[...])


def _merge(x2d, branches, gpre, w_gate, w_branch, w_out, gpost):
    n_tok = x2d.shape[0]
    t = TOKEN_TILE
    tok = lambda w: pl.BlockSpec((t, w), lambda i: (i, 0))
    return pl.pallas_call(
        _merge_kernel,
        grid=(n_tok // t,),
        in_specs=[_row_spec(t, lambda i: i)] + [tok(BRANCH_W)] * 4 + [_whole()] * 5,
        out_specs=_row_spec(t, lambda i: i),
        out_shape=jax.ShapeDtypeStruct((n_tok, D_MODEL), F32),
        compiler_params=_params(1),
        name="merge",
    )(x2d, *branches, gpre, w_gate, w_branch, w_out, gpost)


def _gelu_tanh(x):
    k = -2.0 * LOG2E * math.sqrt(2.0 / math.pi)
    return x / (1.0 + jnp.exp2(x * (k + (k * 0.044715) * (x * x))))


def _ffn_kernel(xp_ref, x_ref, xn_ref, gpre_ref, wup_ref, cw_ref, cb_ref, wdn_ref, gpost_ref, o_ref,
                h_scr, a_scr, *, tiles_per_seq):
    t = TOKEN_TILE
    n_b = t // 8
    pos = pl.program_id(0) % tiles_per_seq
    gpre = gpre_ref[...]
    x = jnp.swapaxes(x_ref[...].reshape(8, n_b, D_MODEL), 0, 1).reshape(t, D_MODEL)
    h_scr[0:t, :] = _rmsnorm(x, gpre).astype(BF16)
    keep_prev = (pos != 0).astype(F32)
    keep_next = (pos != tiles_per_seq - 1).astype(F32)
    hp = pltpu.roll(_rmsnorm(xp_ref[...], gpre) * keep_prev, 1, 0)
    hn = pltpu.roll(_rmsnorm(xn_ref[...], gpre) * keep_next, 1, 0)
    hrow = lax.broadcasted_iota(jnp.int32, (HALO, D_MODEL), 0)
    h_scr[t:, :] = jnp.where(hrow == 0, hp, jnp.where(hrow == 1, hn, 0.0)).astype(BF16)
    h = h_scr[...]
    sub = lax.broadcasted_iota(jnp.int32, (8, FFN_CHUNK), 0)

    def conv(c0):
        u = jnp.dot(h, wup_ref[:, c0:c0 + FFN_CHUNK], preferred_element_type=F32)
        halo = u[t:t + 8]
        first_prev = jnp.where(sub == 0, halo[0:1], pltpu.roll(u[t - 8:t], 1, 0))
        last_next = jnp.where(sub == 7, halo[1:2], pltpu.roll(u[0:8], 7, 0))
        prev = jnp.concatenate([first_prev, u[0:t - 8]], axis=0)
        nxt = jnp.concatenate([u[8:t], last_next], axis=0)
        w = cw_ref[:, c0:c0 + FFN_CHUNK]
        return cb_ref[:, c0:c0 + FFN_CHUNK] + prev * w[0:1] + u[0:t] * w[1:2] + nxt * w[2:3]

    for ch in range(D_FF // FFN_CHUNK):
        gate = conv(ch * FFN_CHUNK)
        val = conv(D_FF + ch * FFN_CHUNK)
        a_scr[:, ch * FFN_CHUNK:(ch + 1) * FFN_CHUNK] = (_gelu_tanh(gate) * val).astype(BF16)
    half = t // 2
    out = jnp.concatenate(
        [x[r0:r0 + half] + _rmsnorm(jnp.dot(a_scr[r0:r0 + half, :], wdn_ref[...], preferred_element_type=F32),
                                    gpost_ref[...]) for r0 in (0, half)], axis=0)
    out = jnp.swapaxes(out.reshape(n_b, 8, D_MODEL), 0, 1).reshape(t, D_MODEL)
    o_ref[...] = out


def _ffn(x2d, gpre, w_up, conv_w, conv_b, w_down, gpost, seq):
    n_tok = x2d.shape[0]
    t = TOKEN_TILE
    per_tile = t // HALO
    n_halo_blocks = n_tok // HALO
    return pl.pallas_call(
        functools.partial(_ffn_kernel, tiles_per_seq=seq // t),
        grid=(n_tok // t,),
        in_specs=[
            _row_spec(HALO, lambda i: jnp.maximum(i * per_tile - 1, 0)),
            _row_spec(t, lambda i: i),
            _row_spec(HALO, lambda i: jnp.minimum((i + 1) * per_tile, n_halo_blocks - 1)),
        ] + [_whole()] * 6,
        out_specs=_row_spec(t, lambda i: i),
        out_shape=jax.ShapeDtypeStruct((n_tok, D_MODEL), F32),
        scratch_shapes=[
            pltpu.VMEM((t + HALO, D_MODEL), BF16),
            pltpu.VMEM((t, D_FF), BF16),
        ],
        compiler_params=_params(1),
        name="conv_ffn",
    )(x2d, x2d, x2d, gpre, w_up, conv_w, conv_b, w_down, gpost)


def _rope_tables(seq):
    t = np.arange(seq)
    axis_dim = HEAD_DIM // 2
    inv = jnp.asarray(ROPE_THETA, F32) ** (-jnp.arange(0, axis_dim, 2, dtype=F32) / axis_dim)
    ang_r = jnp.asarray(t // GRID_W, F32)[:, None] * inv
    ang_c = jnp.asarray(t % GRID_W, F32)[:, None] * inv
    cos = jnp.concatenate([jnp.cos(ang_r)] * 2 + [jnp.cos(ang_c)] * 2, axis=-1)
    sin = jnp.concatenate([-jnp.sin(ang_r), jnp.sin(ang_r), -jnp.sin(ang_c), jnp.sin(ang_c)], axis=-1)
    return jnp.tile(cos, (1, 2)), jnp.tile(sin, (1, 2))


def _group_major(w, axis):
    shape = w.shape
    w = w.reshape(shape[:axis] + (2, 2, HEAD_DIM) + shape[axis + 1:])
    return jnp.swapaxes(w, axis, axis + 1).reshape(shape)


def kernel(x_prompt, x_sample, norm_mix_pre, norm_mix_post, norm_ffn_pre, norm_ffn_post, w_in, na_rpb, gqa_q_norm, gqa_k_norm, diff_lambda_q1, diff_lambda_k1, diff_lambda_q2, diff_lambda_k2, diff_subln, swa_sink, w_branch, w_out, ffn_w_up, ffn_conv_w, ffn_conv_b, ffn_w_down):
    depth = w_in.shape[0]
    w_qkv = jnp.concatenate([
        w_in[:, :, :B_Q], _group_major(w_in[:, :, B_Q:B_K], 2), w_in[:, :, B_K:D_Q],
        _group_major(w_in[:, :, D_Q:D_K], 2), w_in[:, :, D_K:QKV_COLS]], axis=-1).astype(BF16)
    w_gate = w_in[:, :, QKV_COLS:].astype(BF16)
    w_br = w_branch.reshape(depth, N_BRANCH, BRANCH_W, D_MODEL)
    w_br = jnp.stack([w_br[:, 0], _group_major(w_br[:, 1], 1), w_br[:, 2], _group_major(w_br[:, 3], 1)],
                     axis=1).astype(BF16)
    diff_qf, diff_kf = _diff_bias_factors()
    w_o = w_out.astype(BF16)
    w_up = ffn_w_up.astype(BF16)
    w_dn = ffn_w_down.astype(BF16)
    row2 = lambda a: a.astype(F32).reshape(depth, 1, -1)
    tile2 = lambda a: jnp.tile(a.astype(F32), (1, 2)).reshape(depth, 1, -1)
    g_mix_pre, g_mix_post = row2(norm_mix_pre), row2(norm_mix_post)
    g_ffn_pre, g_ffn_post = row2(norm_ffn_pre), row2(norm_ffn_post)
    qn, kn, subln = tile2(gqa_q_norm), tile2(gqa_k_norm), tile2(diff_subln)
    lq1, lk1, lq2, lk2 = (row2(a) for a in (diff_lambda_q1, diff_lambda_k1, diff_lambda_q2, diff_lambda_k2))
    conv_b = row2(ffn_conv_b)
    conv_w = ffn_conv_w.astype(F32)
    sink = swa_sink.astype(F32)

    def run_trunk(x):
        batch, seq, _ = x.shape
        cos_t, sin_t = _rope_tables(seq)
        na_bias = _na_bias_tables(na_rpb.astype(F32), seq // GRID_W)
        x2d = x.reshape(batch * seq, D_MODEL)
        for l in range(depth):
            lambda_init = 0.8 - 0.6 * math.exp(-0.3 * l)
            qkv = _in_proj(x2d, g_mix_pre[l], w_qkv[l], cos_t, sin_t, qn[l], kn[l], seq)
            qkv = qkv.reshape(batch, seq, QKV_COLS)
            o_a = _na_attention(qkv, na_bias[l], batch, seq)
            o_b = _gqa_attention(qkv, batch, seq)
            o_c = _diff_attention(qkv, diff_qf, diff_kf, lq1[l], lk1[l], lq2[l], lk2[l], subln[l], batch, seq,
                                  lambda_init)
            o_d = _swa_attention(qkv, sink[l], batch, seq)
            branches = [o.reshape(batch * seq, BRANCH_W) for o in (o_a, o_b, o_c, o_d)]
            x2d = _merge(x2d, branches, g_mix_pre[l], w_gate[l], w_br[l], w_o[l], g_mix_post[l])
            x2d = _ffn(x2d, g_ffn_pre[l], w_up[l], conv_w[l], conv_b[l], w_dn[l], g_ffn_post[l], seq)
        return x2d.reshape(batch, seq, D_MODEL)

    return (run_trunk(x_prompt), run_trunk(x_sample))
```

```python
import functools
import math

import numpy as np
import jax
import jax.numpy as jnp
from jax import lax
from jax.experimental import pallas as pl
from jax.experimental.pallas import tpu as pltpu

F32 = jnp.float32
BF16 = jnp.bfloat16

D_MODEL = 1024
GRID_W = 64
HEAD_DIM = 64
EPS = 1e-6
NEG_INF = -1e30
LOG2E = math.log2(math.e)
NA_WIN_R = 8
NA_WIN_C = 16
ROPE_THETA = 10000.0
DIFF_QK_DIM = 32
SWA_WINDOW = 128
N_ALIBI_HEADS = 8
N_BRANCH = 4
BRANCH_W = 256
D_FF = 2816
QKV_COLS = 2560

A_Q, A_K, A_V = 0, 256, 512
B_Q, B_K, B_V = 768, 1024, 1152
C_Q, C_K, C_V = 1280, 1536, 1792
D_Q, D_K, D_V = 2048, 2304, 2432

LANES = 128
VMEM_LIMIT = 56 * 1024 * 1024

TOKEN_TILE = 1024
HALO = 16
FFN_CHUNK = 256
NA_Q_ROWS = 4
NA_K_ROWS = 12
GQA_TQ, GQA_TK = 512, 2048
DIFF_TQ, DIFF_TK = 256, 2048
SWA_TQ = 256


def _alibi_slopes():
    s = 2.0 ** (-8.0 * np.arange(1, N_ALIBI_HEADS + 1) / N_ALIBI_HEADS)
    return [float(v) for v in s[0::2]], [float(v) for v in s[1::2]]


DIFF_SLOPES, SWA_SLOPES = _alibi_slopes()


def _params(n_parallel):
    return pltpu.CompilerParams(dimension_semantics=("parallel",) * n_parallel,
                                vmem_limit_bytes=VMEM_LIMIT)


def _whole(space=pltpu.VMEM):
    return pl.BlockSpec(memory_space=space)


def _row_spec(rows, index):
    return pl.BlockSpec((rows, D_MODEL), lambda i: (index(i), 0))


def _rmsnorm(x, gain):
    ms = jnp.mean(x * x, axis=-1, keepdims=True)
    return x * lax.rsqrt(ms + EPS) * gain


def _pair_rms_scale(y):
    lo = lax.broadcasted_iota(jnp.int32, y.shape, 1) < HEAD_DIM
    y2 = y * y
    s_lo = jnp.sum(jnp.where(lo, y2, 0.0), axis=-1, keepdims=True)
    s_hi = jnp.sum(jnp.where(lo, 0.0, y2), axis=-1, keepdims=True)
    ms = jnp.where(lo, s_lo, s_hi) * (1.0 / HEAD_DIM)
    return lax.rsqrt(ms + EPS)


def _edge_case(t, n_t):
    return jnp.where(t == 0, 0, jnp.where(t == n_t - 1, 2, 1))


def _nt_dot(a, b):
    return lax.dot_general(a, b, (((1,), (1,)), ((), ())), preferred_element_type=F32)


def _lo_lanes(rows):
    return lax.broadcasted_iota(jnp.int32, (rows, LANES), 1) < HEAD_DIM


def _own(lo, half):
    return lo if half == 0 else jnp.logical_not(lo)


def _pv_with_rowsum(p, v, own):
    return jnp.dot(p.astype(BF16), jnp.where(own, v, jnp.ones_like(v)), preferred_element_type=F32)


def _normalise(acc):
    return acc / pltpu.roll(acc, HEAD_DIM, 1)


def _in_proj_kernel(x_ref, gain_ref, w_ref, cos_ref, sin_ref, qn_ref, kn_ref, o_ref):
    h = _rmsnorm(x_ref[...], gain_ref[...]).astype(BF16)

    def proj(c0, width):
        return jnp.dot(h, w_ref[:, c0:c0 + width], preferred_element_type=F32)

    def plain(c0, width, scale=None):
        y = proj(c0, width)
        if scale is not None:
            y = y * scale
        o_ref[:, c0:c0 + width] = y.astype(BF16)

    def normed_rope(c0, gain_ref_, scale=None):
        y = proj(c0, LANES)
        yn = y * _pair_rms_scale(y) * gain_ref_[...]
        lane = lax.broadcasted_iota(jnp.int32, yn.shape, 1)
        partner = jnp.where((lane % 32) < 16, pltpu.roll(yn, LANES - 16, 1), pltpu.roll(yn, 16, 1))
        out = yn * cos_ref[...] + partner * sin_ref[...]
        if scale is not None:
            out = out * scale
        o_ref[:, c0:c0 + LANES] = out.astype(BF16)

    hd_scale = HEAD_DIM ** -0.5 * LOG2E
    plain(A_Q, 256, hd_scale)
    plain(A_K, 512)
    normed_rope(B_Q, qn_ref, hd_scale)
    normed_rope(B_Q + LANES, qn_ref, hd_scale)
    normed_rope(B_K, kn_ref)
    plain(B_V, 128)
    plain(C_Q, 256, DIFF_QK_DIM ** -0.5 * LOG2E)
    plain(C_K, 512)
    plain(D_Q, 256, hd_scale)
    plain(D_K, 256)


def _in_proj(x2d, gain, w_qkv, cos_t, sin_t, qn, kn, seq):
    n_tok = x2d.shape[0]
    t = TOKEN_TILE
    tiles_per_seq = seq // t
    return pl.pallas_call(
        _in_proj_kernel,
        grid=(n_tok // t,),
        in_specs=[
            _row_spec(t, lambda i: i),
            _whole(), _whole(),
            pl.BlockSpec((t, LANES), lambda i: (i % tiles_per_seq, 0)),
            pl.BlockSpec((t, LANES), lambda i: (i % tiles_per_seq, 0)),
            _whole(), _whole(),
        ],
        out_specs=pl.BlockSpec((t, QKV_COLS), lambda i: (i, 0)),
        out_shape=jax.ShapeDtypeStruct((n_tok, QKV_COLS), BF16),
        compiler_params=_params(1),
        name="in_proj",
    )(x2d, gain, w_qkv, cos_t, sin_t, qn, kn)


def _na_kernel(q_ref, k_ref, v_ref, b_ref, o_ref, *, rows):
    t = pl.program_id(1)
    case = _edge_case(t, rows // NA_Q_ROWS)
    tq = NA_Q_ROWS * GRID_W
    tk = NA_K_ROWS * GRID_W
    r_start = jnp.clip(t * NA_Q_ROWS - NA_WIN_R // 2, 0, rows - NA_K_ROWS)
    k0 = pl.multiple_of(r_start * GRID_W, GRID_W)
    q = q_ref[0]
    kw = k_ref[0, pl.ds(k0, tk), :]
    vw = v_ref[0, pl.ds(k0, tk), :]
    lo_q, lo_k = _lo_lanes(tq), _lo_lanes(tk)
    for pair in range(2):
        sl = slice(pair * LANES, (pair + 1) * LANES)
        qp, kp, vp = q[:, sl], kw[:, sl], vw[:, sl]
        zero = jnp.zeros_like(qp)
        s_all = _nt_dot(jnp.concatenate([jnp.where(lo_q, qp, zero), jnp.where(lo_q, zero, qp)], axis=0), kp)
        halves = []
        for half in range(2):
            s = s_all[half * tq:(half + 1) * tq] + b_ref[case, 2 * pair + half]
            e = jnp.exp2(s - jnp.max(s, axis=-1, keepdims=True))
            halves.append(_normalise(_pv_with_rowsum(e, vp, _own(lo_k, half))))
        o_ref[0, :, sl] = jnp.where(lo_q, halves[0], halves[1]).astype(BF16)


def _na_bias_tables(rpb, rows):
    depth, heads = rpb.shape[0], rpb.shape[1]
    qc, kc = np.arange(GRID_W)[:, None], np.arange(GRID_W)[None, :]
    c0 = np.clip(qc - NA_WIN_C // 2, 0, GRID_W - NA_WIN_C)
    col_valid = (kc >= c0) & (kc < c0 + NA_WIN_C)
    dc = np.clip(kc - qc + NA_WIN_C - 1, 0, 2 * NA_WIN_C - 2)
    blocks = jnp.where(col_valid, rpb[:, :, :, dc] * LOG2E, NEG_INF).astype(F32)
    masked = jnp.full((depth, heads, GRID_W, GRID_W), NEG_INF, F32)
    cases = []
    for q_row0 in (0, NA_Q_ROWS, rows - NA_Q_ROWS):
        r_start = int(np.clip(q_row0 - NA_WIN_R // 2, 0, rows - NA_K_ROWS))
        strips = []
        for i in range(NA_Q_ROWS):
            qr = q_row0 + i
            r0 = int(np.clip(qr - NA_WIN_R // 2, 0, rows - NA_WIN_R))
            strip = []
            for j in range(NA_K_ROWS):
                kr = r_start + j
                strip.append(blocks[:, :, kr - qr + NA_WIN_R - 1] if r0 <= kr < r0 + NA_WIN_R else masked)
            strips.append(jnp.concatenate(strip, axis=-1))
        cases.append(jnp.concatenate(strips, axis=-2))
    return jnp.stack(cases, axis=1)


def _online_softmax_step(s, pv_fn, carry):
    m, acc = carry
    m_new = jnp.maximum(m, jnp.max(s, axis=-1, keepdims=True))
    alpha = jnp.exp2(m - m_new)
    return m_new, alpha * acc + pv_fn(jnp.exp2(s - m_new))


def _softmax_init(rows):
    return (jnp.full((rows, 1), NEG_INF, F32), jnp.zeros((rows, LANES), F32))


def _stack_group(q, sel):
    g0, g1 = q[:, :LANES], q[:, LANES:]
    zero = jnp.zeros_like(g0)
    return jnp.concatenate([jnp.where(sel, g0, zero), jnp.where(sel, g1, zero)], axis=0)


def _gqa_kernel(q_ref, k_ref, v_ref, o_ref, *, seq):
    tq, tk = GQA_TQ, GQA_TK
    q = q_ref[0]
    lo_q, lo_k = _lo_lanes(tq), _lo_lanes(tk)
    qq = [_stack_group(q, _own(lo_q, h)) for h in range(2)]

    def body(j, carry):
        k0 = pl.multiple_of(j * tk, tk)
        kj = k_ref[0, pl.ds(k0, tk), :]
        vj = v_ref[0, pl.ds(k0, tk), :]
        return tuple(
            _online_softmax_step(_nt_dot(qq[h], kj),
                                 functools.partial(_pv_with_rowsum, v=vj, own=_own(lo_k, h)), carry[h])
            for h in range(2))

    res = lax.fori_loop(0, seq // tk, body, (_softmax_init(2 * tq), _softmax_init(2 * tq)), unroll=2)
    n = [_normalise(acc) for _, acc in res]
    o_ref[0, :, :LANES] = jnp.where(lo_q, n[0][:tq], n[1][:tq]).astype(BF16)
    o_ref[0, :, LANES:] = jnp.where(lo_q, n[0][tq:], n[1][tq:]).astype(BF16)


def _gqa_attention(qkv, batch, seq):
    tq = GQA_TQ
    return pl.pallas_call(
        functools.partial(_gqa_kernel, seq=seq),
        grid=(batch, seq // tq),
        in_specs=[
            pl.BlockSpec((1, tq, 256), lambda b, t: (b, t, B_Q // 256)),
            pl.BlockSpec((1, seq, LANES), lambda b, t: (b, 0, B_K // LANES)),
            pl.BlockSpec((1, seq, LANES), lambda b, t: (b, 0, B_V // LANES)),
        ],
        out_specs=pl.BlockSpec((1, tq, BRANCH_W), lambda b, t: (b, t, 0)),
        out_shape=jax.ShapeDtypeStruct((batch, seq, BRANCH_W), BF16),
        compiler_params=_params(2),
        name="gqa_attention",
    )(qkv, qkv, qkv)


_AUG_KEY_LO, _AUG_KEY_HI, _AUG_ROW, _AUG_DELTA = 0, 3, 6, 9
_AUG_MASK = 12


def _bf16_pieces(x):
    x = np.asarray(x, np.float32)
    pieces = []
    for _ in range(3):
        p = x.astype(BF16).astype(np.float32)
        pieces.append(p)
        x = x - p
    return pieces


def _diff_bias_factors():
    tq, tk = DIFF_TQ, DIFF_TK
    q_side = np.zeros((4, tq, LANES), np.float32)
    for h, slope in enumerate(DIFF_SLOPES):
        c = np.float32(slope * LOG2E)
        c3 = _bf16_pieces(c)
        r3 = _bf16_pieces(-c * np.arange(tq, dtype=np.float32))
        for n in range(3):
            for g in (_AUG_KEY_LO, _AUG_KEY_HI, _AUG_DELTA):
                q_side[h, :, g + n] = c3[n]
            q_side[h, :, _AUG_ROW + n] = r3[n]
        q_side[h, :, _AUG_MASK] = 1.0
    k_side = np.zeros((tk, LANES), np.float32)
    j = np.arange(tk)
    for n in range(3):
        k_side[:, _AUG_KEY_LO + n] = j % 256
        k_side[:, _AUG_KEY_HI + n] = 256 * (j // 256)
        k_side[:, _AUG_ROW + n] = 1.0
    return jnp.asarray(q_side, BF16), jnp.asarray(k_side, F32)


def _diff_kernel(q_ref, k_ref, v_ref, qf_ref, kf_ref, lq1_ref, lk1_ref, lq2_ref, lk2_ref, subln_ref, o_ref, *,
                 seq, lambda_init):
    tq, tk = DIFF_TQ, DIFF_TK
    q_start = pl.multiple_of(pl.program_id(1) * tq, tq)
    q = q_ref[0]
    lam = (jnp.exp(jnp.sum(lq1_ref[...] * lk1_ref[...], axis=-1, keepdims=True))
           - jnp.exp(jnp.sum(lq2_ref[...] * lk2_ref[...], axis=-1, keepdims=True)) + lambda_init)
    lane = lax.broadcasted_iota(jnp.int32, (tq, LANES), 1)
    seg = lane // DIFF_QK_DIM
    lo_q, lo_k = lane < HEAD_DIM, _lo_lanes(tk)
    lane_k = lax.broadcasted_iota(jnp.int32, (tk, LANES), 1)
    key_k = lax.broadcasted_iota(jnp.int32, (tk, LANES), 0)
    delta_lanes = (lane_k >= _AUG_DELTA) & (lane_k < _AUG_DELTA + 3)
    mask_lane = lane_k == _AUG_MASK
    own_dist = jnp.abs(lax.broadcasted_iota(jnp.int32, (tq, tq), 0)
                       - lax.broadcasted_iota(jnp.int32, (tq, tq), 1)).astype(F32)

    def pv(p, v, lo):
        return jnp.concatenate([_pv_with_rowsum(p[:2 * tq], v, lo),
                                _pv_with_rowsum(p[2 * tq:], v, jnp.logical_not(lo))], axis=0)

    qq_aug, carry = [], []
    for pair in range(2):
        sl = slice(pair * LANES, (pair + 1) * LANES)
        qp = q[:, sl]
        zero = jnp.zeros_like(qp)
        rows4 = jnp.concatenate([jnp.where(seg == i, qp, zero) for i in range(4)], axis=0)
        factors = jnp.concatenate([qf_ref[2 * pair]] * 2 + [qf_ref[2 * pair + 1]] * 2, axis=0)
        qq_aug.append(jnp.concatenate([rows4, factors], axis=1))
        bias = jnp.concatenate([(DIFF_SLOPES[2 * pair] * LOG2E) * own_dist] * 2
                               + [(DIFF_SLOPES[2 * pair + 1] * LOG2E) * own_dist] * 2, axis=0)
        s = _nt_dot(rows4, k_ref[0, pl.ds(q_start, tq), sl]) - bias
        m = jnp.max(s, axis=-1, keepdims=True)
        carry.append((m, pv(jnp.exp2(s - m), v_ref[0, pl.ds(q_start, tq), sl], lo_q)))

    def body(j, carry):
        k0 = pl.multiple_of(j * tk, tk)
        rel = key_k + (k0 - q_start)
        sign = jnp.where(rel < 0, 1.0, -1.0)
        kf = sign * (kf_ref[...] - jnp.where(delta_lanes, (q_start - k0).astype(F32), 0.0))
        own_key = jnp.where(rel >= 0, jnp.where(rel < tq, NEG_INF, 0.0), 0.0)
        k_factors = jnp.where(mask_lane, own_key, kf).astype(BF16)
        new = []
        for pair in range(2):
            sl = slice(pair * LANES, (pair + 1) * LANES)
            s = _nt_dot(qq_aug[pair], jnp.concatenate([k_ref[0, pl.ds(k0, tk), sl], k_factors], axis=1))
            new.append(_online_softmax_step(
                s, functools.partial(pv, v=v_ref[0, pl.ds(k0, tk), sl], lo=lo_k), carry[pair]))
        return tuple(new)

    carry = lax.fori_loop(0, seq // tk, body, tuple(carry), unroll=2)
    for pair in range(2):
        p = _normalise(carry[pair][1])
        head_a = p[0:tq] - lam * p[tq:2 * tq]
        head_b = p[2 * tq:3 * tq] - lam * p[3 * tq:4 * tq]
        o = jnp.where(lo_q, head_a, head_b)
        o = o * _pair_rms_scale(o) * subln_ref[...] * (1.0 - lambda_init)
        o_ref[0, :, pair * LANES:(pair + 1) * LANES] = o.astype(BF16)


def _diff_attention(qkv, q_factors, k_factors, lq1, lk1, lq2, lk2, subln, batch, seq, lambda_init):
    tq = DIFF_TQ
    return pl.pallas_call(
        functools.partial(_diff_kernel, seq=seq, lambda_init=lambda_init),
        grid=(batch, seq // tq),
        in_specs=[
            pl.BlockSpec((1, tq, 256), lambda b, t: (b, t, C_Q // 256)),
            pl.BlockSpec((1, seq, 256), lambda b, t: (b, 0, C_K // 256)),
            pl.BlockSpec((1, seq, 256), lambda b, t: (b, 0, C_V // 256)),
        ] + [_whole()] * 7,
        out_specs=pl.BlockSpec((1, tq, BRANCH_W), lambda b, t: (b, t, 0)),
        out_shape=jax.ShapeDtypeStruct((batch, seq, BRANCH_W), BF16),
        compiler_params=_params(2),
        name="diff_attention",
    )(qkv, qkv, qkv, q_factors, k_factors, lq1, lk1, lq2, lk2, subln)


def _swa_kernel(q_ref, k_ref, v_ref, b_ref, sink_ref, o_ref, *, seq):
    tq = SWA_TQ
    span = tq + 2 * SWA_WINDOW
    q_start = pl.program_id(1) * tq
    k_start = pl.multiple_of(jnp.clip(q_start - SWA_WINDOW, 0, seq - span), SWA_WINDOW)
    case = _edge_case(pl.program_id(1), seq // tq)
    q = q_ref[0]
    kw = k_ref[0, pl.ds(k_start, span), :]
    vw = v_ref[0, pl.ds(k_start, span), :]
    lo_q, lo_k = _lo_lanes(tq), _lo_lanes(span)
    outs = [[None, None], [None, None]]
    for kv_head in range(2):
        s_all = _nt_dot(_stack_group(q, _own(lo_q, kv_head)), kw)
        for g in range(2):
            head = 2 * kv_head + g
            sink = sink_ref[head] * LOG2E
            s = s_all[g * tq:(g + 1) * tq] + b_ref[case, head]
            m = jnp.maximum(jnp.max(s, axis=-1, keepdims=True), sink)
            r = _pv_with_rowsum(jnp.exp2(s - m), vw, _own(lo_k, kv_head))
            outs[g][kv_head] = r / (pltpu.roll(r, HEAD_DIM, 1) + jnp.exp2(sink - m))
    o_ref[0, :, :LANES] = jnp.where(lo_q, outs[0][0], outs[0][1]).astype(BF16)
    o_ref[0, :, LANES:] = jnp.where(lo_q, outs[1][0], outs[1][1]).astype(BF16)


def _swa_bias_tables():
    tq = SWA_TQ
    span = tq + 2 * SWA_WINDOW
    i, j = np.arange(tq)[:, None], np.arange(span)[None, :]
    cases = []
    for lead in (0, SWA_WINDOW, 2 * SWA_WINDOW):
        rel = np.abs(j - i - lead)
        cases.append(np.stack([np.where(rel <= SWA_WINDOW, -(slope * LOG2E) * rel, NEG_INF) for slope in SWA_SLOPES]))
    return jnp.asarray(np.stack(cases), F32)


def _local_kernel(aq_ref, ak_ref, av_ref, ab_ref, dq_ref, dk_ref, dv_ref, db_ref, sink_ref, oa_ref, od_ref, *,
                  rows, seq):
    _na_kernel(aq_ref, ak_ref, av_ref, ab_ref, oa_ref, rows=rows)
    _swa_kernel(dq_ref, dk_ref, dv_ref, db_ref, sink_ref, od_ref, seq=seq)


def _local_attention(qkv, na_bias, sink, batch, seq):
    assert NA_Q_ROWS * GRID_W == SWA_TQ
    tq = SWA_TQ
    out = jax.ShapeDtypeStruct((batch, seq, BRANCH_W), BF16)
    q_spec = lambda c0: pl.BlockSpec((1, tq, 256), lambda b, t: (b, t, c0 // 256))
    kv_spec = lambda c0, w: pl.BlockSpec((1, seq, w), lambda b, t: (b, 0, c0 // w))
    o_spec = pl.BlockSpec((1, tq, BRANCH_W), lambda b, t: (b, t, 0))
    return pl.pallas_call(
        functools.partial(_local_kernel, rows=seq // GRID_W, seq=seq),
        grid=(batch, seq // tq),
        in_specs=[q_spec(A_Q), kv_spec(A_K, 256), kv_spec(A_V, 256), _whole(),
                  q_spec(D_Q), kv_spec(D_K, LANES), kv_spec(D_V, LANES), _whole(), _whole(pltpu.SMEM)],
        out_specs=[o_spec, o_spec],
        out_shape=[out, out],
        compiler_params=_params(2),
        name="local_attention",
    )(qkv, qkv, qkv, na_bias, qkv, qkv, qkv, _swa_bias_tables(), sink)


def _merge_kernel(x_ref, oa_ref, ob_ref, oc_ref, od_ref, gpre_ref, wg_ref, wb_ref, wo_ref, gpost_ref, o_ref):
    x = x_ref[...]
    h = _rmsnorm(x, gpre_ref[...]).astype(BF16)
    merged = None
    for i, br_ref in enumerate((oa_ref, ob_ref, oc_ref, od_ref)):
        gate = jax.nn.sigmoid(jnp.dot(h, wg_ref[:, i * D_MODEL:(i + 1) * D_MODEL], preferred_element_type=F32))
        term = gate * jnp.dot(br_ref[...], wb_ref[i], preferred_element_type=F32)
        merged = term if merged is None else merged + term
    half = x.shape[0] // 2
    for r0 in (0, half):
        mix = jnp.dot(merged[r0:r0 + half].astype(BF16), wo_ref[...], preferred_element_type=F32)
        o_ref[r0:r0 + half, :] = x[r0:r0 + half] + _rmsnorm(mix, gpost_ref[...])


def _merge(x2d, branches, gpre, w_gate, w_branch, w_out, gpost):
    n_tok = x2d.shape[0]
    t = TOKEN_TILE
    tok = lambda w: pl.BlockSpec((t, w), lambda i: (i, 0))
    return pl.pallas_call(
        _merge_kernel,
        grid=(n_tok // t,),
        in_specs=[_row_spec(t, lambda i: i)] + [tok(BRANCH_W)] * 4 + [_whole()] * 5,
        out_specs=_row_spec(t, lambda i: i),
        out_shape=jax.ShapeDtypeStruct((n_tok, D_MODEL), F32),
        compiler_params=_params(1),
        name="merge",
    )(x2d, *branches, gpre, w_gate, w_branch, w_out, gpost)


def _gelu_tanh(x):
    k = -2.0 * LOG2E * math.sqrt(2.0 / math.pi)
    return x / (1.0 + jnp.exp2(x * (k + (k * 0.044715) * (x * x))))


def _ffn_kernel(xp_ref, x_ref, xn_ref, gpre_ref, wup_ref, cw_ref, cb_ref, wdn_ref, gpost_ref, o_ref,
                h_scr, a_scr, *, tiles_per_seq):
    t = TOKEN_TILE
    n_b = t // 8
    pos = pl.program_id(0) % tiles_per_seq
    gpre = gpre_ref[...]
    x = jnp.swapaxes(x_ref[...].reshape(8, n_b, D_MODEL), 0, 1).reshape(t, D_MODEL)
    h_scr[0:t, :] = _rmsnorm(x, gpre).astype(BF16)
    keep_prev = (pos != 0).astype(F32)
    keep_next = (pos != tiles_per_seq - 1).astype(F32)
    hp = pltpu.roll(_rmsnorm(xp_ref[...], gpre) * keep_prev, 1, 0)
    hn = pltpu.roll(_rmsnorm(xn_ref[...], gpre) * keep_next, 1, 0)
    hrow = lax.broadcasted_iota(jnp.int32, (HALO, D_MODEL), 0)
    h_scr[t:, :] = jnp.where(hrow == 0, hp, jnp.where(hrow == 1, hn, 0.0)).astype(BF16)
    h = h_scr[...]
    sub = lax.broadcasted_iota(jnp.int32, (8, FFN_CHUNK), 0)

    def conv(c0):
        u = jnp.dot(h, wup_ref[:, c0:c0 + FFN_CHUNK], preferred_element_type=F32)
        halo = u[t:t + 8]
        first_prev = jnp.where(sub == 0, halo[0:1], pltpu.roll(u[t - 8:t], 1, 0))
        last_next = jnp.where(sub == 7, halo[1:2], pltpu.roll(u[0:8], 7, 0))
        prev = jnp.concatenate([first_prev, u[0:t - 8]], axis=0)
        nxt = jnp.concatenate([u[8:t], last_next], axis=0)
        w = cw_ref[:, c0:c0 + FFN_CHUNK]
        return cb_ref[:, c0:c0 + FFN_CHUNK] + prev * w[0:1] + u[0:t] * w[1:2] + nxt * w[2:3]

    for ch in range(D_FF // FFN_CHUNK):
        gate = conv(ch * FFN_CHUNK)
        val = conv(D_FF + ch * FFN_CHUNK)
        a_scr[:, ch * FFN_CHUNK:(ch + 1) * FFN_CHUNK] = (_gelu_tanh(gate) * val).astype(BF16)
    half = t // 2
    out = jnp.concatenate(
        [x[r0:r0 + half] + _rmsnorm(jnp.dot(a_scr[r0:r0 + half, :], wdn_ref[...], preferred_element_type=F32),
                                    gpost_ref[...]) for r0 in (0, half)], axis=0)
    out = jnp.swapaxes(out.reshape(n_b, 8, D_MODEL), 0, 1).reshape(t, D_MODEL)
    o_ref[...] = out


def _ffn(x2d, gpre, w_up, conv_w, conv_b, w_down, gpost, seq):
    n_tok = x2d.shape[0]
    t = TOKEN_TILE
    per_tile = t // HALO
    n_halo_blocks = n_tok // HALO
    return pl.pallas_call(
        functools.partial(_ffn_kernel, tiles_per_seq=seq // t),
        grid=(n_tok // t,),
        in_specs=[
            _row_spec(HALO, lambda i: jnp.maximum(i * per_tile - 1, 0)),
            _row_spec(t, lambda i: i),
            _row_spec(HALO, lambda i: jnp.minimum((i + 1) * per_tile, n_halo_blocks - 1)),
        ] + [_whole()] * 6,
        out_specs=_row_spec(t, lambda i: i),
        out_shape=jax.ShapeDtypeStruct((n_tok, D_MODEL), F32),
        scratch_shapes=[
            pltpu.VMEM((t + HALO, D_MODEL), BF16),
            pltpu.VMEM((t, D_FF), BF16),
        ],
        compiler_params=_params(1),
        name="conv_ffn",
    )(x2d, x2d, x2d, gpre, w_up, conv_w, conv_b, w_down, gpost)


def _rope_tables(seq):
    t = np.arange(seq)
    axis_dim = HEAD_DIM // 2
    inv = jnp.asarray(ROPE_THETA, F32) ** (-jnp.arange(0, axis_dim, 2, dtype=F32) / axis_dim)
    ang_r = jnp.asarray(t // GRID_W, F32)[:, None] * inv
    ang_c = jnp.asarray(t % GRID_W, F32)[:, None] * inv
    cos = jnp.concatenate([jnp.cos(ang_r)] * 2 + [jnp.cos(ang_c)] * 2, axis=-1)
    sin = jnp.concatenate([-jnp.sin(ang_r), jnp.sin(ang_r), -jnp.sin(ang_c), jnp.sin(ang_c)], axis=-1)
    return jnp.tile(cos, (1, 2)), jnp.tile(sin, (1, 2))


def _group_major(w, axis):
    shape = w.shape
    w = w.reshape(shape[:axis] + (2, 2, HEAD_DIM) + shape[axis + 1:])
    return jnp.swapaxes(w, axis, axis + 1).reshape(shape)


def kernel(x_prompt, x_sample, norm_mix_pre, norm_mix_post, norm_ffn_pre, norm_ffn_post, w_in, na_rpb, gqa_q_norm, gqa_k_norm, diff_lambda_q1, diff_lambda_k1, diff_lambda_q2, diff_lambda_k2, diff_subln, swa_sink, w_branch, w_out, ffn_w_up, ffn_conv_w, ffn_conv_b, ffn_w_down):
    depth = w_in.shape[0]
    w_qkv = jnp.concatenate([
        w_in[:, :, :B_Q], _group_major(w_in[:, :, B_Q:B_K], 2), w_in[:, :, B_K:D_Q],
        _group_major(w_in[:, :, D_Q:D_K], 2), w_in[:, :, D_K:QKV_COLS]], axis=-1).astype(BF16)
    w_gate = w_in[:, :, QKV_COLS:].astype(BF16)
    w_br = w_branch.reshape(depth, N_BRANCH, BRANCH_W, D_MODEL)
    w_br = jnp.stack([w_br[:, 0], _group_major(w_br[:, 1], 1), w_br[:, 2], _group_major(w_br[:, 3], 1)],
                     axis=1).astype(BF16)
    diff_qf, diff_kf = _diff_bias_factors()
    w_o = w_out.astype(BF16)
    w_up = ffn_w_up.astype(BF16)
    w_dn = ffn_w_down.astype(BF16)
    row2 = lambda a: a.astype(F32).reshape(depth, 1, -1)
    tile2 = lambda a: jnp.tile(a.astype(F32), (1, 2)).reshape(depth, 1, -1)
    g_mix_pre, g_mix_post = row2(norm_mix_pre), row2(norm_mix_post)
    g_ffn_pre, g_ffn_post = row2(norm_ffn_pre), row2(norm_ffn_post)
    qn, kn, subln = tile2(gqa_q_norm), tile2(gqa_k_norm), tile2(diff_subln)
    lq1, lk1, lq2, lk2 = (row2(a) for a in (diff_lambda_q1, diff_lambda_k1, diff_lambda_q2, diff_lambda_k2))
    conv_b = row2(ffn_conv_b)
    conv_w = ffn_conv_w.astype(F32)
    sink = swa_sink.astype(F32)

    def run_trunk(x):
        batch, seq, _ = x.shape
        cos_t, sin_t = _rope_tables(seq)
        na_bias = _na_bias_tables(na_rpb.astype(F32), seq // GRID_W)
        x2d = x.reshape(batch * seq, D_MODEL)
        for l in range(depth):
            lambda_init = 0.8 - 0.6 * math.exp(-0.3 * l)
            qkv = _in_proj(x2d, g_mix_pre[l], w_qkv[l], cos_t, sin_t, qn[l], kn[l], seq)
            qkv = qkv.reshape(batch, seq, QKV_COLS)
            o_a, o_d = _local_attention(qkv, na_bias[l], sink[l], batch, seq)
            o_b = _gqa_attention(qkv, batch, seq)
            o_c = _diff_attention(qkv, diff_qf, diff_kf, lq1[l], lk1[l], lq2[l], lk2[l], subln[l], batch, seq,
                                  lambda_init)
            branches = [o.reshape(batch * seq, BRANCH_W) for o in (o_a, o_b, o_c, o_d)]
            x2d = _merge(x2d, branches, g_mix_pre[l], w_gate[l], w_br[l], w_o[l], g_mix_post[l])
            x2d = _ffn(x2d, g_ffn_pre[l], w_up[l], conv_w[l], conv_b[l], w_dn[l], g_ffn_post[l], seq)
        return x2d.reshape(batch, seq, D_MODEL)

    return (run_trunk(x_prompt), run_trunk(x_sample))
```
